```python
import math
import jax, jax.numpy as jnp
from jax import lax
import numpy as np

D_MODEL = 1024
BATCH = 4
SEQ = 4096
DEPTH = 4

N_MIXERS = 2
D_FF = 2816
CONV_WIDTH = 31
N_HEADS = 8
HEAD_DIM = 64
QK_WIDTH = N_HEADS * 2 * HEAD_DIM
V_WIDTH = N_HEADS * 2 * HEAD_DIM
Q_BLOCK = 128
RMS_EPS = 1e-6
SUBLN_EPS = 1e-5
LN_EPS = 1e-5
N_CONV_LAYERS = (DEPTH + 1) // 2
N_ATTN_LAYERS = DEPTH // 2

kernel_name = "hybrid_conformer_diffattn_macaron_encoder"


def _rmsnorm(x, g, eps=RMS_EPS):
    xf = x.astype(jnp.float32)
    y = xf * lax.rsqrt(jnp.mean(xf * xf, axis=-1, keepdims=True) + eps)
    return (y * g.astype(jnp.float32)).astype(x.dtype)


def _layernorm(x, g, b, eps=LN_EPS):
    xf = x.astype(jnp.float32)
    mu = jnp.mean(xf, axis=-1, keepdims=True)
    var = jnp.mean(jnp.square(xf - mu), axis=-1, keepdims=True)
    y = (xf - mu) * lax.rsqrt(var + eps)
    return (y * g.astype(jnp.float32) + b.astype(jnp.float32)).astype(x.dtype)


def _half_ffn(x, g_pre, g_post, w_gate, w_up, w_down):
    h = _rmsnorm(x, g_pre)
    f = (jax.nn.silu(h @ w_gate) * (h @ w_up)) @ w_down
    return x + 0.5 * _rmsnorm(f, g_post)


def _conformer_conv(h, w_pw1, b_pw1, w_dw, b_dw, ln_g, ln_b, w_pw2, b_pw2):
    u = h @ w_pw1 + b_pw1
    u = u[..., :D_MODEL] * jax.nn.sigmoid(u[..., D_MODEL:])
    pad = (CONV_WIDTH - 1) // 2
    c = lax.conv_general_dilated(
        u, w_dw[:, None, :].astype(u.dtype),
        window_strides=(1,), padding=[(pad, pad)],
        dimension_numbers=("NWC", "WIO", "NWC"),
        feature_group_count=D_MODEL) + b_dw
    c = jax.nn.silu(_layernorm(c, ln_g, ln_b))
    return c @ w_pw2 + b_pw2


def _lambda_init(layer_idx):
    return 0.8 - 0.6 * math.exp(-0.3 * layer_idx)


def _diff_attention(h, w_qkv, w_o, lq1, lk1, lq2, lk2, subln_g, lambda_init):
    B, S, _ = h.shape
    qkv = h @ w_qkv
    q, k, v = jnp.split(qkv, [QK_WIDTH, 2 * QK_WIDTH], axis=-1)
    q = q.reshape(B, S, N_HEADS, 2, HEAD_DIM)
    k = k.reshape(B, S, N_HEADS, 2, HEAD_DIM)
    v = v.reshape(B, S, N_HEADS, 2 * HEAD_DIM)
    f32 = jnp.float32
    lam = (jnp.exp(jnp.sum(lq1.astype(f32) * lk1.astype(f32)))
           - jnp.exp(jnp.sum(lq2.astype(f32) * lk2.astype(f32))) + lambda_init)
    slopes = 2.0 ** (-8.0 * (jnp.arange(N_HEADS, dtype=f32) + 1.0) / N_HEADS)
    scale = HEAD_DIM ** -0.5
    kpos = jnp.arange(S, dtype=f32)
    nblk = S // Q_BLOCK
    q_blocks = q.reshape(B, nblk, Q_BLOCK, N_HEADS, 2, HEAD_DIM).transpose(1, 0, 2, 3, 4, 5)
    starts = jnp.arange(nblk, dtype=jnp.int32) * Q_BLOCK

    def one_block(args):
        qb, start = args
        s = jnp.einsum("bqhmd,bkhmd->bhmqk", qb, k).astype(f32) * scale
        qpos = start.astype(f32) + jnp.arange(Q_BLOCK, dtype=f32)
        dist = jnp.abs(qpos[:, None] - kpos[None, :])
        s = s - slopes[:, None, None, None] * dist
        p = jax.nn.softmax(s, axis=-1)
        a = p[:, :, 0] - lam * p[:, :, 1]
        return jnp.einsum("bhqk,bkhe->bqhe", a.astype(v.dtype), v)

    o = lax.map(one_block, (q_blocks, starts))
    o = o.transpose(1, 0, 2, 3, 4).reshape(B, S, N_HEADS, 2 * HEAD_DIM)
    o = _rmsnorm(o, subln_g, SUBLN_EPS) * (1.0 - lambda_init)
    return o.reshape(B, S, V_WIDTH) @ w_o


def setup_inputs(seed: int = 0) -> dict:
    key = jax.random.key(seed)
    ks = jax.random.split(key, 24)
    D, F, K = D_MODEL, D_FF, CONV_WIDTH
    NC, NA = N_CONV_LAYERS, N_ATTN_LAYERS
    nrm = jax.random.normal
    f32 = jnp.float32

    def gain(k, shape):
        return 1.0 + 0.05 * nrm(k, shape, f32)

    def small(k, shape):
        return 0.02 * nrm(k, shape, f32)

    return {
        "x": nrm(ks[0], (BATCH, SEQ, D), f32),
        "ffn_norm_pre": gain(ks[1], (DEPTH, 2, D)),
        "ffn_norm_post": gain(ks[2], (DEPTH, 2, D)),
        "ffn_w_gate": nrm(ks[3], (DEPTH, 2, D, F), f32) * D ** -0.5,
        "ffn_w_up": nrm(ks[4], (DEPTH, 2, D, F), f32) * D ** -0.5,
        "ffn_w_down": nrm(ks[5], (DEPTH, 2, F, D), f32) * F ** -0.5,
        "mix_norm_pre": gain(ks[6], (DEPTH, D)),
        "mix_norm_post": gain(ks[7], (DEPTH, D)),
        "conv_w_pw1": nrm(ks[8], (NC, D, 2 * D), f32) * D ** -0.5,
        "conv_b_pw1": small(ks[9], (NC, 2 * D)),
        "conv_w_dw": nrm(ks[10], (NC, K, D), f32) * K ** -0.5,
        "conv_b_dw": small(ks[11], (NC, D)),
        "conv_ln_g": gain(ks[12], (NC, D)),
        "conv_ln_b": small(ks[13], (NC, D)),
        "conv_w_pw2": nrm(ks[14], (NC, D, D), f32) * D ** -0.5,
        "conv_b_pw2": small(ks[15], (NC, D)),
        "attn_w_qkv": nrm(ks[16], (NA, D, 2 * QK_WIDTH + V_WIDTH), f32) * D ** -0.5,
        "attn_w_o": nrm(ks[17], (NA, V_WIDTH, D), f32) * V_WIDTH ** -0.5,
        "attn_lambda_q1": 0.1 * nrm(ks[18], (NA, HEAD_DIM), f32),
        "attn_lambda_k1": 0.1 * nrm(ks[19], (NA, HEAD_DIM), f32),
        "attn_lambda_q2": 0.1 * nrm(ks[20], (NA, HEAD_DIM), f32),
        "attn_lambda_k2": 0.1 * nrm(ks[21], (NA, HEAD_DIM), f32),
        "attn_subln_g": gain(ks[22], (NA, 2 * HEAD_DIM)),
    }


def reference(x, ffn_norm_pre, ffn_norm_post, ffn_w_gate, ffn_w_up, ffn_w_down,
              mix_norm_pre, mix_norm_post,
              conv_w_pw1, conv_b_pw1, conv_w_dw, conv_b_dw, conv_ln_g, conv_ln_b,
              conv_w_pw2, conv_b_pw2,
              attn_w_qkv, attn_w_o, attn_lambda_q1, attn_lambda_k1,
              attn_lambda_q2, attn_lambda_k2, attn_subln_g):
    ic = 0
    ia = 0
    for i in range(DEPTH):
        x = _half_ffn(x, ffn_norm_pre[i, 0], ffn_norm_post[i, 0],
                      ffn_w_gate[i, 0], ffn_w_up[i, 0], ffn_w_down[i, 0])
        h = _rmsnorm(x, mix_norm_pre[i])
        if i % N_MIXERS == 0:
            m = _conformer_conv(h, conv_w_pw1[ic], conv_b_pw1[ic], conv_w_dw[ic],
                                conv_b_dw[ic], conv_ln_g[ic], conv_ln_b[ic],
                                conv_w_pw2[ic], conv_b_pw2[ic])
            ic += 1
        else:
            m = _diff_attention(h, attn_w_qkv[ia], attn_w_o[ia],
                                attn_lambda_q1[ia], attn_lambda_k1[ia],
                                attn_lambda_q2[ia], attn_lambda_k2[ia],
                                attn_subln_g[ia], _lambda_init(i))
            ia += 1
        x = x + _rmsnorm(m, mix_norm_post[i])
        x = _half_ffn(x, ffn_norm_pre[i, 1], ffn_norm_post[i, 1],
                      ffn_w_gate[i, 1], ffn_w_up[i, 1], ffn_w_down[i, 1])
    return x
```

```python
import functools
import math

import jax
import jax.numpy as jnp
from jax import lax
from jax.experimental import pallas as pl
from jax.experimental.pallas import tpu as pltpu

D_MODEL = 1024
D_FF = 2816
CONV_WIDTH = 31
CONV_PAD = (CONV_WIDTH - 1) // 2
N_HEADS = 8
HEAD_DIM = 64
HEAD_WIDTH = 2 * HEAD_DIM
QK_WIDTH = N_HEADS * HEAD_WIDTH
RMS_EPS = 1e-6
SUBLN_EPS = 1e-5
LN_EPS = 1e-5

V7X_VMEM_BYTES = 64 * 1024 * 1024
VMEM_LIMIT_BYTES = V7X_VMEM_BYTES * 7 // 8

ROW_TILE = 512
FF_CHUNK = 256
HALO_ROWS = 16
CONV_ROWS = 32
CONV_COLS = 256
SUBLANES = 8
SHIFT_ROWS = ROW_TILE + 2 * HALO_ROWS - SUBLANES
Q_TILE = 256
K_TILE = 512

F32 = jnp.float32
BF16 = jnp.bfloat16


def _rms(x, g, eps):
    return x * lax.rsqrt(jnp.mean(x * x, axis=-1, keepdims=True) + eps) * g


def _const_spec(shape):
    nd = len(shape)
    return pl.BlockSpec(shape, lambda *_: (0,) * nd, pipeline_mode=pl.Buffered(1))


def _params(n_axes):
    return pltpu.CompilerParams(dimension_semantics=("arbitrary",) * n_axes,
                                vmem_limit_bytes=VMEM_LIMIT_BYTES)


def _ffn_kernel(x_ref, gpre_ref, gpost_ref, wg_ref, wu_ref, wd_ref, o_ref, p_ref):
    x = x_ref[...]
    h = _rms(x, gpre_ref[...], RMS_EPS).astype(BF16)
    for c in range(D_FF // FF_CHUNK):
        cols = slice(c * FF_CHUNK, (c + 1) * FF_CHUNK)
        g = jnp.dot(h, wg_ref[:, cols], preferred_element_type=F32)
        u = jnp.dot(h, wu_ref[:, cols], preferred_element_type=F32)
        p_ref[:, cols] = (g * jax.nn.sigmoid(g) * u).astype(BF16)
    f = jnp.dot(p_ref[...], wd_ref[...], preferred_element_type=F32)
    o_ref[...] = x + 0.5 * _rms(f, gpost_ref[...], RMS_EPS)


def _ffn(x, g_pre, g_post, w_gate, w_up, w_down):
    m = x.shape[0]
    row = pl.BlockSpec((ROW_TILE, D_MODEL), lambda i: (i, 0))
    return pl.pallas_call(
        _ffn_kernel,
        grid=(m // ROW_TILE,),
        in_specs=[row, _const_spec((1, D_MODEL)), _const_spec((1, D_MODEL)),
                  _const_spec((D_MODEL, D_FF)), _const_spec((D_MODEL, D_FF)),
                  _const_spec((D_FF, D_MODEL))],
        out_specs=row,
        out_shape=jax.ShapeDtypeStruct((m, D_MODEL), F32),
        scratch_shapes=[pltpu.VMEM((ROW_TILE, D_FF), BF16)],
        compiler_params=_params(1),
        name="ffn",
    )(x, g_pre, g_post, w_gate, w_up, w_down)


def _glu_kernel(x_ref, g_ref, w_ref, b_ref, u_ref):
    h = _rms(x_ref[...], g_ref[...], RMS_EPS).astype(BF16)
    a = jnp.dot(h, w_ref[:, :D_MODEL], preferred_element_type=F32) + b_ref[:, :D_MODEL]
    t = jnp.dot(h, w_ref[:, D_MODEL:], preferred_element_type=F32) + b_ref[:, D_MODEL:]
    u_ref[...] = a * jax.nn.sigmoid(t)


def _glu(x, g, w_pw1, b_pw1):
    m = x.shape[0]
    row = pl.BlockSpec((ROW_TILE, D_MODEL), lambda i: (i, 0))
    return pl.pallas_call(
        _glu_kernel,
        grid=(m // ROW_TILE,),
        in_specs=[row, _const_spec((1, D_MODEL)), _const_spec((D_MODEL, 2 * D_MODEL)),
                  _const_spec((1, 2 * D_MODEL))],
        out_specs=row,
        out_shape=jax.ShapeDtypeStruct((m, D_MODEL), F32),
        compiler_params=_params(1),
        name="conv_glu",
    )(x, g, w_pw1, b_pw1)


def _conv_kernel(x_ref, u_ref, prev_ref, next_ref, wdw_ref, bdw_ref, lng_ref, lnb_ref,
                 w2_ref, b2_ref, gpost_ref, o_ref, ubuf_ref, sh_ref, c_ref):
    i = pl.program_id(1)
    n = pl.num_programs(1)
    ubuf_ref[:HALO_ROWS, :] = jnp.where(i > 0, prev_ref[...], 0.0)
    ubuf_ref[HALO_ROWS:HALO_ROWS + ROW_TILE, :] = u_ref[...]
    ubuf_ref[HALO_ROWS + ROW_TILE:, :] = jnp.where(i < n - 1, next_ref[...], 0.0)

    for cb in range(D_MODEL // CONV_COLS):
        cols = slice(cb * CONV_COLS, (cb + 1) * CONV_COLS)
        for phase in range(SUBLANES):
            sh_ref[phase] = ubuf_ref[phase:phase + SHIFT_ROWS, cols]

        def row_block(r, carry, cols=cols):
            base = pl.multiple_of(r * CONV_ROWS, CONV_ROWS)
            acc = jnp.zeros((CONV_ROWS, CONV_COLS), F32)
            for k in range(CONV_WIDTH):
                step, phase = divmod(HALO_ROWS - CONV_PAD + k, SUBLANES)
                win = sh_ref[phase, pl.ds(base + step * SUBLANES, CONV_ROWS), :]
                acc = acc + win * wdw_ref[k:k + 1, cols]
            c_ref[pl.ds(base, CONV_ROWS), cols] = acc
            return carry

        lax.fori_loop(0, ROW_TILE // CONV_ROWS, row_block, 0)

    c = c_ref[...] + bdw_ref[...]
    mu = jnp.mean(c, axis=-1, keepdims=True)
    d = c - mu
    var = jnp.mean(d * d, axis=-1, keepdims=True)
    y = d * lax.rsqrt(var + LN_EPS) * lng_ref[...] + lnb_ref[...]
    y = (y * jax.nn.sigmoid(y)).astype(BF16)
    mix = jnp.dot(y, w2_ref[...], preferred_element_type=F32) + b2_ref[...]
    o_ref[...] = x_ref[...] + _rms(mix, gpost_ref[...], RMS_EPS)


def _conv(x, u, w_dw, b_dw, ln_g, ln_b, w_pw2, b_pw2, g_post):
    b, s, _ = x.shape
    tiles = s // ROW_TILE
    halo_per_tile = ROW_TILE // HALO_ROWS
    last_halo = s // HALO_ROWS - 1
    row = pl.BlockSpec((None, ROW_TILE, D_MODEL), lambda bi, i: (bi, i, 0))
    prev = pl.BlockSpec((None, HALO_ROWS, D_MODEL),
                        lambda bi, i: (bi, jnp.maximum(i * halo_per_tile - 1, 0), 0))
    nxt = pl.BlockSpec((None, HALO_ROWS, D_MODEL),
                       lambda bi, i: (bi, jnp.minimum((i + 1) * halo_per_tile, last_halo), 0))
    vec = _const_spec((1, D_MODEL))
    return pl.pallas_call(
        _conv_kernel,
        grid=(b, tiles),
        in_specs=[row, row, prev, nxt, _const_spec((CONV_WIDTH, D_MODEL)), vec, vec, vec,
                  _const_spec((D_MODEL, D_MODEL)), vec, vec],
        out_specs=row,
        out_shape=jax.ShapeDtypeStruct((b, s, D_MODEL), F32),
        scratch_shapes=[pltpu.VMEM((ROW_TILE + 2 * HALO_ROWS, D_MODEL), F32),
                        pltpu.VMEM((SUBLANES, SHIFT_ROWS, CONV_COLS), F32),
                        pltpu.VMEM((ROW_TILE, D_MODEL), F32)],
        compiler_params=_params(2),
        name="conv_mix",
    )(x, u, u, u, w_dw, b_dw, ln_g, ln_b, w_pw2, b_pw2, g_post)


def _qkv_kernel(x_ref, g_ref, w_ref, o_ref):
    h = _rms(x_ref[...], g_ref[...], RMS_EPS).astype(BF16)
    scale = HEAD_DIM ** -0.5
    for part in range(3):
        cols = slice(part * QK_WIDTH, (part + 1) * QK_WIDTH)
        y = jnp.dot(h, w_ref[:, cols], preferred_element_type=F32)
        if part == 0:
            y = y * scale
        o_ref[:, cols] = y.astype(BF16)


def _qkv(x, g, w_qkv):
    m = x.shape[0]
    return pl.pallas_call(
        _qkv_kernel,
        grid=(m // ROW_TILE,),
        in_specs=[pl.BlockSpec((ROW_TILE, D_MODEL), lambda i: (i, 0)),
                  _const_spec((1, D_MODEL)), _const_spec((D_MODEL, 3 * QK_WIDTH))],
        out_specs=pl.BlockSpec((ROW_TILE, 3 * QK_WIDTH), lambda i: (i, 0)),
        out_shape=jax.ShapeDtypeStruct((m, 3 * QK_WIDTH), BF16),
        compiler_params=_params(1),
        name="attn_qkv",
    )(x, g, w_qkv)


def _attn_kernel(slope_ref, lam_ref, subg_ref, q_ref, k_ref, v_ref, o_ref,
                 m_ref, l_ref, acc_ref, *, lambda_init, seq):
    qi = pl.program_id(2)
    q = q_ref[...]
    lane = lax.broadcasted_iota(jnp.int32, q.shape, 1)
    zero = jnp.zeros_like(q)
    qmaps = (jnp.where(lane < HEAD_DIM, q, zero), jnp.where(lane >= HEAD_DIM, q, zero))
    slope = slope_ref[:, :1]
    rel = (qi * Q_TILE + lax.broadcasted_iota(jnp.int32, (Q_TILE, K_TILE), 0)
           - lax.broadcasted_iota(jnp.int32, (Q_TILE, K_TILE), 1))

    m_ref[...] = jnp.full(m_ref.shape, -1e30, F32)
    l_ref[...] = jnp.zeros(l_ref.shape, F32)
    acc_ref[...] = jnp.zeros(acc_ref.shape, F32)

    def key_block(c, carry):
        start = pl.multiple_of(c * K_TILE, K_TILE)
        k = k_ref[pl.ds(start, K_TILE), :]
        v = v_ref[pl.ds(start, K_TILE), :]
        bias = jnp.abs(rel - c * K_TILE).astype(F32) * slope
        for mp in range(2):
            s = lax.dot_general(qmaps[mp], k, (((1,), (1,)), ((), ())),
                                preferred_element_type=F32) - bias
            m_old = m_ref[mp]
            m_new = jnp.maximum(m_old, jnp.max(s, axis=-1, keepdims=True))
            alpha = jnp.exp(m_old - m_new)
            p = jnp.exp(s - m_new)
            l_ref[mp] = alpha * l_ref[mp] + jnp.sum(p, axis=-1, keepdims=True)
            acc_ref[mp] = alpha * acc_ref[mp] + jnp.dot(p.astype(BF16), v,
                                                        preferred_element_type=F32)
            m_ref[mp] = m_new
        return carry

    lax.fori_loop(0, seq // K_TILE, key_block, 0)

    lq = lam_ref[...]
    lam = (jnp.exp(jnp.sum(lq[0:1] * lq[1:2], axis=-1, keepdims=True))
           - jnp.exp(jnp.sum(lq[2:3] * lq[3:4], axis=-1, keepdims=True)) + lambda_init)
    o = acc_ref[0] / l_ref[0] - lam * (acc_ref[1] / l_ref[1])
    o = _rms(o, subg_ref[...], SUBLN_EPS) * (1.0 - lambda_init)
    o_ref[...] = o.astype(BF16)


def _attn(qkv, slopes, lam_vecs, subln_g, lambda_init):
    b, s, _ = qkv.shape
    kernel = functools.partial(_attn_kernel, lambda_init=lambda_init, seq=s)
    return pl.pallas_call(
        kernel,
        grid=(b, N_HEADS, s // Q_TILE),
        in_specs=[pl.BlockSpec((None, 1, HEAD_WIDTH), lambda bi, h, i: (h, 0, 0)),
                  _const_spec((4, HEAD_DIM)), _const_spec((1, HEAD_WIDTH)),
                  pl.BlockSpec((None, Q_TILE, HEAD_WIDTH), lambda bi, h, i: (bi, i, h)),
                  pl.BlockSpec((None, s, HEAD_WIDTH), lambda bi, h, i: (bi, 0, N_HEADS + h)),
                  pl.BlockSpec((None, s, HEAD_WIDTH), lambda bi, h, i: (bi, 0, 2 * N_HEADS + h))],
        out_specs=pl.BlockSpec((None, Q_TILE, HEAD_WIDTH), lambda bi, h, i: (bi, i, h)),
        out_shape=jax.ShapeDtypeStruct((b, s, QK_WIDTH), BF16),
        scratch_shapes=[pltpu.VMEM((2, Q_TILE, 1), F32), pltpu.VMEM((2, Q_TILE, 1), F32),
                        pltpu.VMEM((2, Q_TILE, HEAD_WIDTH), F32)],
        compiler_params=_params(3),
        name="attn_core",
    )(slopes, lam_vecs, subln_g, qkv, qkv, qkv)


def _oproj_kernel(x_ref, o_ref, w_ref, g_ref, y_ref):
    mix = jnp.dot(o_ref[...], w_ref[...], preferred_element_type=F32)
    y_ref[...] = x_ref[...] + _rms(mix, g_ref[...], RMS_EPS)


def _oproj(x, o, w_o, g_post):
    m = x.shape[0]
    row = pl.BlockSpec((ROW_TILE, D_MODEL), lambda i: (i, 0))
    return pl.pallas_call(
        _oproj_kernel,
        grid=(m // ROW_TILE,),
        in_specs=[row, row, _const_spec((QK_WIDTH, D_MODEL)), _const_spec((1, D_MODEL))],
        out_specs=row,
        out_shape=jax.ShapeDtypeStruct((m, D_MODEL), F32),
        compiler_params=_params(1),
        name="attn_oproj",
    )(x, o, w_o, g_post)


def _lambda_init(layer_idx):
    return 0.8 - 0.6 * math.exp(-0.3 * layer_idx)


def kernel(x, ffn_norm_pre, ffn_norm_post, ffn_w_gate, ffn_w_up, ffn_w_down, mix_norm_pre, mix_norm_post, conv_w_pw1, conv_b_pw1, conv_w_dw, conv_b_dw, conv_ln_g, conv_ln_b, conv_w_pw2, conv_b_pw2, attn_w_qkv, attn_w_o, attn_lambda_q1, attn_lambda_k1, attn_lambda_q2, attn_lambda_k2, attn_subln_g):
    b, s, d = x.shape
    depth = ffn_norm_pre.shape[0]
    m = b * s
    assert d == D_MODEL and m % ROW_TILE == 0 and s % ROW_TILE == 0
    assert s % Q_TILE == 0 and s % K_TILE == 0

    w_gate, w_up, w_down = (w.astype(BF16) for w in (ffn_w_gate, ffn_w_up, ffn_w_down))
    w_pw1, w_pw2 = conv_w_pw1.astype(BF16), conv_w_pw2.astype(BF16)
    w_qkv, w_o = attn_w_qkv.astype(BF16), attn_w_o.astype(BF16)
    slopes = 2.0 ** (-8.0 * (jnp.arange(N_HEADS, dtype=F32) + 1.0) / N_HEADS)
    slopes = jnp.broadcast_to(slopes[:, None, None], (N_HEADS, 1, HEAD_WIDTH))

    def vec(a):
        return a.reshape(1, -1)

    x = x.reshape(m, d)
    ic = ia = 0
    for i in range(depth):
        x = _ffn(x, vec(ffn_norm_pre[i, 0]), vec(ffn_norm_post[i, 0]),
                 w_gate[i, 0], w_up[i, 0], w_down[i, 0])
        if i % 2 == 0:
            u = _glu(x, vec(mix_norm_pre[i]), w_pw1[ic], vec(conv_b_pw1[ic]))
            x = _conv(x.reshape(b, s, d), u.reshape(b, s, d), conv_w_dw[ic], vec(conv_b_dw[ic]),
                      vec(conv_ln_g[ic]), vec(conv_ln_b[ic]), w_pw2[ic], vec(conv_b_pw2[ic]),
                      vec(mix_norm_post[i])).reshape(m, d)
            ic += 1
        else:
            qkv = _qkv(x, vec(mix_norm_pre[i]), w_qkv[ia])
            lam_vecs = jnp.stack([attn_lambda_q1[ia], attn_lambda_k1[ia],
                                  attn_lambda_q2[ia], attn_lambda_k2[ia]])
            o = _attn(qkv.reshape(b, s, 3 * QK_WIDTH), slopes, lam_vecs,
                      vec(attn_subln_g[ia]), _lambda_init(i))
            x = _oproj(x, o.reshape(m, QK_WIDTH), w_o[ia], vec(mix_norm_post[i]))
            ia += 1
        x = _ffn(x, vec(ffn_norm_pre[i, 1]), vec(ffn_norm_post[i, 1]),
                 w_gate[i, 1], w_up[i, 1], w_down[i, 1])
    return x.reshape(b, s, d)
```

```python
import functools
import math

import jax
import jax.numpy as jnp
from jax import lax
from jax.experimental import pallas as pl
from jax.experimental.pallas import tpu as pltpu

D_MODEL = 1024
D_FF = 2816
CONV_WIDTH = 31
CONV_PAD = (CONV_WIDTH - 1) // 2
N_HEADS = 8
HEAD_DIM = 64
HEAD_WIDTH = 2 * HEAD_DIM
QK_WIDTH = N_HEADS * HEAD_WIDTH
RMS_EPS = 1e-6
SUBLN_EPS = 1e-5
LN_EPS = 1e-5
LOG2_E = math.log2(math.e)

V7X_VMEM_BYTES = 64 * 1024 * 1024
VMEM_LIMIT_BYTES = V7X_VMEM_BYTES * 7 // 8

ROW_TILE = 512
FF_CHUNK = 256
HALO_ROWS = 16
CONV_ROWS = 32
CONV_COLS = 256
SUBLANES = 8
BF16_ROWS = 16
SHIFT_ROWS = ROW_TILE + 2 * HALO_ROWS - SUBLANES
Q_TILE = 256
K_TILE = 512

F32 = jnp.float32
BF16 = jnp.bfloat16


def _rms(x, g, eps):
    return x * lax.rsqrt(jnp.mean(x * x, axis=-1, keepdims=True) + eps) * g


def _const_spec(shape, lead=()):
    zeros = (0,) * len(shape)
    return pl.BlockSpec((None,) * len(lead) + tuple(shape), lambda *_: tuple(lead) + zeros,
                        pipeline_mode=pl.Buffered(1))


def _params(n_axes):
    return pltpu.CompilerParams(dimension_semantics=("arbitrary",) * n_axes,
                                vmem_limit_bytes=VMEM_LIMIT_BYTES)


def _ffn_kernel(x_ref, gpre_ref, gpost_ref, wg_ref, wu_ref, wd_ref, o_ref, p_ref):
    x = x_ref[...]
    h = _rms(x, gpre_ref[...], RMS_EPS).astype(BF16)
    for c in range(D_FF // FF_CHUNK):
        cols = slice(c * FF_CHUNK, (c + 1) * FF_CHUNK)
        g = jnp.dot(h, wg_ref[:, cols], preferred_element_type=F32)
        u = jnp.dot(h, wu_ref[:, cols], preferred_element_type=F32)
        p_ref[:, cols] = (g * jax.nn.sigmoid(g) * u).astype(BF16)
    f = jnp.dot(p_ref[...], wd_ref[...], preferred_element_type=F32)
    o_ref[...] = x + 0.5 * _rms(f, gpost_ref[...], RMS_EPS)


def _ffn(x, g_pre, g_post, w_gate, w_up, w_down, idx):
    m = x.shape[0]
    row = pl.BlockSpec((ROW_TILE, D_MODEL), lambda i: (i, 0))
    return pl.pallas_call(
        _ffn_kernel,
        grid=(m // ROW_TILE,),
        in_specs=[row, _const_spec((1, D_MODEL), idx), _const_spec((1, D_MODEL), idx),
                  _const_spec((D_MODEL, D_FF), idx), _const_spec((D_MODEL, D_FF), idx),
                  _const_spec((D_FF, D_MODEL), idx)],
        out_specs=row,
        out_shape=jax.ShapeDtypeStruct((m, D_MODEL), F32),
        scratch_shapes=[pltpu.VMEM((ROW_TILE, D_FF), BF16)],
        compiler_params=_params(1),
        name="ffn",
    )(x, g_pre, g_post, w_gate, w_up, w_down)


def _glu_kernel(x_ref, g_ref, w_ref, b_ref, u_ref):
    h = _rms(x_ref[...], g_ref[...], RMS_EPS).astype(BF16)
    a = jnp.dot(h, w_ref[:, :D_MODEL], preferred_element_type=F32) + b_ref[:, :D_MODEL]
    t = jnp.dot(h, w_ref[:, D_MODEL:], preferred_element_type=F32) + b_ref[:, D_MODEL:]
    u_ref[...] = a * jax.nn.sigmoid(t)


def _glu(x, g, w_pw1, b_pw1, layer, ic):
    m = x.shape[0]
    row = pl.BlockSpec((ROW_TILE, D_MODEL), lambda i: (i, 0))
    return pl.pallas_call(
        _glu_kernel,
        grid=(m // ROW_TILE,),
        in_specs=[row, _const_spec((1, D_MODEL), (layer,)),
                  _const_spec((D_MODEL, 2 * D_MODEL), (ic,)),
                  _const_spec((1, 2 * D_MODEL), (ic,))],
        out_specs=row,
        out_shape=jax.ShapeDtypeStruct((m, D_MODEL), F32),
        compiler_params=_params(1),
        name="conv_glu",
    )(x, g, w_pw1, b_pw1)


def _conv_kernel(x_ref, u_ref, prev_ref, next_ref, wdw_ref, bdw_ref, lng_ref, lnb_ref,
                 w2_ref, b2_ref, gpost_ref, o_ref, ubuf_ref, sh_ref, c_ref):
    i = pl.program_id(1)
    n = pl.num_programs(1)
    ubuf_ref[:HALO_ROWS, :] = jnp.where(i > 0, prev_ref[...], 0.0)
    ubuf_ref[HALO_ROWS:HALO_ROWS + ROW_TILE, :] = u_ref[...]
    ubuf_ref[HALO_ROWS + ROW_TILE:, :] = jnp.where(i < n - 1, next_ref[...], 0.0)

    for cb in range(D_MODEL // CONV_COLS):
        cols = slice(cb * CONV_COLS, (cb + 1) * CONV_COLS)
        for phase in range(SUBLANES):
            sh_ref[phase] = ubuf_ref[phase:phase + SHIFT_ROWS, cols]

        def row_block(r, carry, cols=cols):
            base = pl.multiple_of(r * CONV_ROWS, CONV_ROWS)
            acc = jnp.zeros((CONV_ROWS, CONV_COLS), F32)
            for k in range(CONV_WIDTH):
                step, phase = divmod(HALO_ROWS - CONV_PAD + k, SUBLANES)
                win = sh_ref[phase, pl.ds(base + step * SUBLANES, CONV_ROWS), :]
                acc = acc + win * wdw_ref[k:k + 1, cols]
            c_ref[pl.ds(base, CONV_ROWS), cols] = acc
            return carry

        lax.fori_loop(0, ROW_TILE // CONV_ROWS, row_block, 0)

    c = c_ref[...] + bdw_ref[...]
    mu = jnp.mean(c, axis=-1, keepdims=True)
    d = c - mu
    var = jnp.mean(d * d, axis=-1, keepdims=True)
    y = d * lax.rsqrt(var + LN_EPS) * lng_ref[...] + lnb_ref[...]
    y = (y * jax.nn.sigmoid(y)).astype(BF16)
    mix = jnp.dot(y, w2_ref[...], preferred_element_type=F32) + b2_ref[...]
    o_ref[...] = x_ref[...] + _rms(mix, gpost_ref[...], RMS_EPS)


def _conv(x, u, w_dw, b_dw, ln_g, ln_b, w_pw2, b_pw2, g_post, layer, ic):
    b, s, _ = x.shape
    tiles = s // ROW_TILE
    halo_per_tile = ROW_TILE // HALO_ROWS
    last_halo = s // HALO_ROWS - 1
    row = pl.BlockSpec((None, ROW_TILE, D_MODEL), lambda bi, i: (bi, i, 0))
    prev = pl.BlockSpec((None, HALO_ROWS, D_MODEL),
                        lambda bi, i: (bi, jnp.maximum(i * halo_per_tile - 1, 0), 0))
    nxt = pl.BlockSpec((None, HALO_ROWS, D_MODEL),
                       lambda bi, i: (bi, jnp.minimum((i + 1) * halo_per_tile, last_halo), 0))
    vec = _const_spec((1, D_MODEL), (ic,))
    return pl.pallas_call(
        _conv_kernel,
        grid=(b, tiles),
        in_specs=[row, row, prev, nxt, _const_spec((CONV_WIDTH, D_MODEL), (ic,)), vec, vec, vec,
                  _const_spec((D_MODEL, D_MODEL), (ic,)), vec,
                  _const_spec((1, D_MODEL), (layer,))],
        out_specs=row,
        out_shape=jax.ShapeDtypeStruct((b, s, D_MODEL), F32),
        scratch_shapes=[pltpu.VMEM((ROW_TILE + 2 * HALO_ROWS, D_MODEL), F32),
                        pltpu.VMEM((SUBLANES, SHIFT_ROWS, CONV_COLS), F32),
                        pltpu.VMEM((ROW_TILE, D_MODEL), F32)],
        compiler_params=_params(2),
        name="conv_mix",
    )(x, u, u, u, w_dw, b_dw, ln_g, ln_b, w_pw2, b_pw2, g_post)


def _qkv_kernel(x_ref, g_ref, w_ref, o_ref):
    h = _rms(x_ref[...], g_ref[...], RMS_EPS).astype(BF16)
    q_scale = HEAD_DIM ** -0.5 * LOG2_E
    for part in range(3):
        cols = slice(part * QK_WIDTH, (part + 1) * QK_WIDTH)
        y = jnp.dot(h, w_ref[:, cols], preferred_element_type=F32)
        if part == 0:
            y = y * q_scale
        o_ref[:, cols] = y.astype(BF16)


def _qkv(x, g, w_qkv, layer, ia):
    m = x.shape[0]
    return pl.pallas_call(
        _qkv_kernel,
        grid=(m // ROW_TILE,),
        in_specs=[pl.BlockSpec((ROW_TILE, D_MODEL), lambda i: (i, 0)),
                  _const_spec((1, D_MODEL), (layer,)),
                  _const_spec((D_MODEL, 3 * QK_WIDTH), (ia,))],
        out_specs=pl.BlockSpec((ROW_TILE, 3 * QK_WIDTH), lambda i: (i, 0)),
        out_shape=jax.ShapeDtypeStruct((m, 3 * QK_WIDTH), BF16),
        compiler_params=_params(1),
        name="attn_qkv",
    )(x, g, w_qkv)


def _attn_kernel(slope_ref, lam_ref, subg_ref, q_ref, k_ref, v_ref, o_ref,
                 vt_ref, bias_ref, *, lambda_init, seq):
    qi = pl.program_id(2)
    n_chunks = seq // K_TILE
    bias_rows = 2 * seq - Q_TILE

    @pl.when(qi == 0)
    def _per_head_setup():
        ones_row = lax.broadcasted_iota(jnp.int32, (BF16_ROWS, K_TILE), 0) == 0
        for c in range(n_chunks):
            vt_ref[c, :HEAD_WIDTH] = v_ref[c * K_TILE:(c + 1) * K_TILE, :].astype(F32).T.astype(BF16)
            vt_ref[c, HEAD_WIDTH:] = ones_row.astype(BF16)
        slope = slope_ref[:, :1] * LOG2_E
        rel = (lax.broadcasted_iota(jnp.int32, (Q_TILE, Q_TILE), 0)
               - lax.broadcasted_iota(jnp.int32, (Q_TILE, Q_TILE), 1) - (seq - Q_TILE))

        def fill(c, carry):
            r0 = pl.multiple_of(c * Q_TILE, Q_TILE)
            bias_ref[pl.ds(r0, Q_TILE), :] = jnp.abs(rel + r0).astype(F32) * slope
            return carry

        lax.fori_loop(0, bias_rows // Q_TILE, fill, 0)

    q = q_ref[...]
    lane = lax.broadcasted_iota(jnp.int32, q.shape, 1)
    zero = jnp.zeros_like(q)
    qmaps = (jnp.where(lane < HEAD_DIM, q, zero), jnp.where(lane >= HEAD_DIM, q, zero))
    window = (seq - Q_TILE) - qi * Q_TILE

    def raw_scores(c):
        k = k_ref[c * K_TILE:(c + 1) * K_TILE, :]
        return [lax.dot_general(k, qm, (((1,), (1,)), ((), ())), preferred_element_type=F32)
                for qm in qmaps]

    m, acc = [None, None], [None, None]
    qk_next = raw_scores(0)
    for c in range(n_chunks):
        qk = qk_next
        if c + 1 < n_chunks:
            qk_next = raw_scores(c + 1)
        bias = bias_ref[pl.ds(pl.multiple_of(window + c * K_TILE, Q_TILE), K_TILE), :]
        vt = vt_ref[c]
        for mp in range(2):
            s = qk[mp] - bias
            m_new = jnp.max(s, axis=0, keepdims=True)
            if c > 0:
                m_new = jnp.maximum(m[mp], m_new)
            p = jnp.exp2(s - m_new).astype(BF16)
            pv = jnp.dot(vt, p, preferred_element_type=F32)
            if c > 0:
                pv = jnp.exp2(m[mp] - m_new) * acc[mp] + pv
            m[mp], acc[mp] = m_new, pv

    lq = lam_ref[...]
    lam = (jnp.exp(jnp.sum(lq[0:1] * lq[1:2], axis=-1, keepdims=True))
           - jnp.exp(jnp.sum(lq[2:3] * lq[3:4], axis=-1, keepdims=True)) + lambda_init)
    ot = [a[:HEAD_WIDTH] / a[HEAD_WIDTH:HEAD_WIDTH + 1] for a in acc]
    o = (ot[0] - lam * ot[1]).T
    o = _rms(o, subg_ref[...], SUBLN_EPS) * (1.0 - lambda_init)
    o_ref[...] = o.astype(BF16)


def _attn(qkv, slopes, lam_vecs, subln_g, lambda_init, ia):
    b, s, _ = qkv.shape
    kernel = functools.partial(_attn_kernel, lambda_init=lambda_init, seq=s)
    return pl.pallas_call(
        kernel,
        grid=(b, N_HEADS, s // Q_TILE),
        in_specs=[pl.BlockSpec((None, 1, HEAD_WIDTH), lambda bi, h, i: (h, 0, 0)),
                  _const_spec((4, HEAD_DIM), (ia,)), _const_spec((1, HEAD_WIDTH), (ia,)),
                  pl.BlockSpec((None, Q_TILE, HEAD_WIDTH), lambda bi, h, i: (bi, i, h)),
                  pl.BlockSpec((None, s, HEAD_WIDTH), lambda bi, h, i: (bi, 0, N_HEADS + h)),
                  pl.BlockSpec((None, s, HEAD_WIDTH), lambda bi, h, i: (bi, 0, 2 * N_HEADS + h))],
        out_specs=pl.BlockSpec((None, Q_TILE, HEAD_WIDTH), lambda bi, h, i: (bi, i, h)),
        out_shape=jax.ShapeDtypeStruct((b, s, QK_WIDTH), BF16),
        scratch_shapes=[pltpu.VMEM((s // K_TILE, HEAD_WIDTH + BF16_ROWS, K_TILE), BF16),
                        pltpu.VMEM((2 * s - Q_TILE, Q_TILE), F32)],
        compiler_params=_params(3),
        name="attn_core",
    )(slopes, lam_vecs, subln_g, qkv, qkv, qkv)


def _oproj_kernel(x_ref, o_ref, w_ref, g_ref, y_ref):
    mix = jnp.dot(o_ref[...], w_ref[...], preferred_element_type=F32)
    y_ref[...] = x_ref[...] + _rms(mix, g_ref[...], RMS_EPS)


def _oproj(x, o, w_o, g_post, layer, ia):
    m = x.shape[0]
    row = pl.BlockSpec((ROW_TILE, D_MODEL), lambda i: (i, 0))
    return pl.pallas_call(
        _oproj_kernel,
        grid=(m // ROW_TILE,),
        in_specs=[row, row, _const_spec((QK_WIDTH, D_MODEL), (ia,)),
                  _const_spec((1, D_MODEL), (layer,))],
        out_specs=row,
        out_shape=jax.ShapeDtypeStruct((m, D_MODEL), F32),
        compiler_params=_params(1),
        name="attn_oproj",
    )(x, o, w_o, g_post)


def _lambda_init(layer_idx):
    return 0.8 - 0.6 * math.exp(-0.3 * layer_idx)


def kernel(x, ffn_norm_pre, ffn_norm_post, ffn_w_gate, ffn_w_up, ffn_w_down, mix_norm_pre, mix_norm_post, conv_w_pw1, conv_b_pw1, conv_w_dw, conv_b_dw, conv_ln_g, conv_ln_b, conv_w_pw2, conv_b_pw2, attn_w_qkv, attn_w_o, attn_lambda_q1, attn_lambda_k1, attn_lambda_q2, attn_lambda_k2, attn_subln_g):
    b, s, d = x.shape
    depth = ffn_norm_pre.shape[0]
    m = b * s
    assert d == D_MODEL and m % ROW_TILE == 0 and s % ROW_TILE == 0
    assert s % Q_TILE == 0 and s % K_TILE == 0

    w_gate, w_up, w_down = (w.astype(BF16) for w in (ffn_w_gate, ffn_w_up, ffn_w_down))
    w_pw1, w_pw2 = conv_w_pw1.astype(BF16), conv_w_pw2.astype(BF16)
    w_qkv, w_o = attn_w_qkv.astype(BF16), attn_w_o.astype(BF16)
    g_ffn_pre, g_ffn_post = ffn_norm_pre[:, :, None, :], ffn_norm_post[:, :, None, :]
    g_mix_pre, g_mix_post = mix_norm_pre[:, None, :], mix_norm_post[:, None, :]
    b_pw1, b_dw, b_pw2 = conv_b_pw1[:, None, :], conv_b_dw[:, None, :], conv_b_pw2[:, None, :]
    ln_g, ln_b = conv_ln_g[:, None, :], conv_ln_b[:, None, :]
    subln_g = attn_subln_g[:, None, :]
    lam_vecs = jnp.stack([attn_lambda_q1, attn_lambda_k1, attn_lambda_q2, attn_lambda_k2], axis=1)
    slopes = 2.0 ** (-8.0 * (jnp.arange(N_HEADS, dtype=F32) + 1.0) / N_HEADS)
    slopes = jnp.broadcast_to(slopes[:, None, None], (N_HEADS, 1, HEAD_WIDTH))

    x = x.reshape(m, d)
    ic = ia = 0
    for i in range(depth):
        x = _ffn(x, g_ffn_pre, g_ffn_post, w_gate, w_up, w_down, (i, 0))
        if i % 2 == 0:
            u = _glu(x, g_mix_pre, w_pw1, b_pw1, i, ic)
            x = _conv(x.reshape(b, s, d), u.reshape(b, s, d), conv_w_dw, b_dw, ln_g, ln_b,
                      w_pw2, b_pw2, g_mix_post, i, ic).reshape(m, d)
            ic += 1
        else:
            qkv = _qkv(x, g_mix_pre, w_qkv, i, ia)
            o = _attn(qkv.reshape(b, s, 3 * QK_WIDTH), slopes, lam_vecs, subln_g,
                      _lambda_init(i), ia)
            x = _oproj(x, o.reshape(m, QK_WIDTH), w_o, g_mix_post, i, ia)
            ia += 1
        x = _ffn(x, g_ffn_pre, g_ffn_post, w_gate, w_up, w_down, (i, 1))
    return x.reshape(b, s, d)
```

```python
import functools
import math

import jax
import jax.numpy as jnp
from jax import lax
from jax.experimental import pallas as pl
from jax.experimental.pallas import tpu as pltpu

D_MODEL = 1024
D_FF = 2816
CONV_WIDTH = 31
CONV_PAD = (CONV_WIDTH - 1) // 2
N_HEADS = 8
HEAD_DIM = 64
HEAD_WIDTH = 2 * HEAD_DIM
QK_WIDTH = N_HEADS * HEAD_WIDTH
RMS_EPS = 1e-6
SUBLN_EPS = 1e-5
LN_EPS = 1e-5
LOG2_E = math.log2(math.e)
F32_HUGE = 3.0e38

V7X_VMEM_BYTES = 64 * 1024 * 1024
VMEM_LIMIT_BYTES = V7X_VMEM_BYTES * 7 // 8

ROW_TILE = 512
FF_CHUNK = 256
HALO_ROWS = 16
CONV_ROWS = 32
CONV_COLS = 256
SUBLANES = 8
BF16_ROWS = 16
SHIFT_ROWS = ROW_TILE + 2 * HALO_ROWS - SUBLANES
Q_TILE = 512
K_TILE = 512
SCORE_LOOKAHEAD = 1
BIAS_SPLIT = 3
POS_RADIX = 256
assert Q_TILE == K_TILE and K_TILE <= POS_RADIX * POS_RADIX

F32 = jnp.float32
BF16 = jnp.bfloat16


def _rms(x, g, eps):
    return x * lax.rsqrt(jnp.mean(x * x, axis=-1, keepdims=True) + eps) * g


def _const_spec(shape, lead=()):
    zeros = (0,) * len(shape)
    return pl.BlockSpec((None,) * len(lead) + tuple(shape), lambda *_: tuple(lead) + zeros,
                        pipeline_mode=pl.Buffered(1))


def _params(n_axes):
    return pltpu.CompilerParams(dimension_semantics=("arbitrary",) * n_axes,
                                vmem_limit_bytes=VMEM_LIMIT_BYTES)


def _ffn_kernel(x_ref, gpre_ref, gpost_ref, wg_ref, wu_ref, wd_ref, o_ref, p_ref):
    x = x_ref[...]
    h = _rms(x, gpre_ref[...], RMS_EPS).astype(BF16)
    for c in range(D_FF // FF_CHUNK):
        cols = slice(c * FF_CHUNK, (c + 1) * FF_CHUNK)
        g = jnp.dot(h, wg_ref[:, cols], preferred_element_type=F32)
        u = jnp.dot(h, wu_ref[:, cols], preferred_element_type=F32)
        p_ref[:, cols] = (g * jax.nn.sigmoid(g) * u).astype(BF16)
    f = jnp.dot(p_ref[...], wd_ref[...], preferred_element_type=F32)
    o_ref[...] = x + 0.5 * _rms(f, gpost_ref[...], RMS_EPS)


def _ffn(x, g_pre, g_post, w_gate, w_up, w_down, idx):
    m = x.shape[0]
    row = pl.BlockSpec((ROW_TILE, D_MODEL), lambda i: (i, 0))
    return pl.pallas_call(
        _ffn_kernel,
        grid=(m // ROW_TILE,),
        in_specs=[row, _const_spec((1, D_MODEL), idx), _const_spec((1, D_MODEL), idx),
                  _const_spec((D_MODEL, D_FF), idx), _const_spec((D_MODEL, D_FF), idx),
                  _const_spec((D_FF, D_MODEL), idx)],
        out_specs=row,
        out_shape=jax.ShapeDtypeStruct((m, D_MODEL), F32),
        scratch_shapes=[pltpu.VMEM((ROW_TILE, D_FF), BF16)],
        compiler_params=_params(1),
        name="ffn",
    )(x, g_pre, g_post, w_gate, w_up, w_down)


def _glu_kernel(x_ref, g_ref, w_ref, b_ref, u_ref):
    h = _rms(x_ref[...], g_ref[...], RMS_EPS).astype(BF16)
    a = jnp.dot(h, w_ref[:, :D_MODEL], preferred_element_type=F32) + b_ref[:, :D_MODEL]
    t = jnp.dot(h, w_ref[:, D_MODEL:], preferred_element_type=F32) + b_ref[:, D_MODEL:]
    u_ref[...] = a * jax.nn.sigmoid(t)


def _glu(x, g, w_pw1, b_pw1, layer, ic):
    m = x.shape[0]
    row = pl.BlockSpec((ROW_TILE, D_MODEL), lambda i: (i, 0))
    return pl.pallas_call(
        _glu_kernel,
        grid=(m // ROW_TILE,),
        in_specs=[row, _const_spec((1, D_MODEL), (layer,)),
                  _const_spec((D_MODEL, 2 * D_MODEL), (ic,)),
                  _const_spec((1, 2 * D_MODEL), (ic,))],
        out_specs=row,
        out_shape=jax.ShapeDtypeStruct((m, D_MODEL), F32),
        compiler_params=_params(1),
        name="conv_glu",
    )(x, g, w_pw1, b_pw1)


def _conv_kernel(x_ref, u_ref, prev_ref, next_ref, wdw_ref, bdw_ref, lng_ref, lnb_ref,
                 w2_ref, b2_ref, gpost_ref, o_ref, ubuf_ref, sh_ref, c_ref):
    i = pl.program_id(1)
    n = pl.num_programs(1)
    ubuf_ref[:HALO_ROWS, :] = jnp.where(i > 0, prev_ref[...], 0.0)
    ubuf_ref[HALO_ROWS:HALO_ROWS + ROW_TILE, :] = u_ref[...]
    ubuf_ref[HALO_ROWS + ROW_TILE:, :] = jnp.where(i < n - 1, next_ref[...], 0.0)

    for cb in range(D_MODEL // CONV_COLS):
        cols = slice(cb * CONV_COLS, (cb + 1) * CONV_COLS)
        for phase in range(SUBLANES):
            sh_ref[phase] = ubuf_ref[phase:phase + SHIFT_ROWS, cols]

        def row_block(r, carry, cols=cols):
            base = pl.multiple_of(r * CONV_ROWS, CONV_ROWS)
            acc = jnp.zeros((CONV_ROWS, CONV_COLS), F32)
            for k in range(CONV_WIDTH):
                step, phase = divmod(HALO_ROWS - CONV_PAD + k, SUBLANES)
                win = sh_ref[phase, pl.ds(base + step * SUBLANES, CONV_ROWS), :]
                acc = acc + win * wdw_ref[k:k + 1, cols]
            c_ref[pl.ds(base, CONV_ROWS), cols] = acc
            return carry

        lax.fori_loop(0, ROW_TILE // CONV_ROWS, row_block, 0)

    c = c_ref[...] + bdw_ref[...]
    mu = jnp.mean(c, axis=-1, keepdims=True)
    d = c - mu
    var = jnp.mean(d * d, axis=-1, keepdims=True)
    y = d * lax.rsqrt(var + LN_EPS) * lng_ref[...] + lnb_ref[...]
    y = (y * jax.nn.sigmoid(y)).astype(BF16)
    mix = jnp.dot(y, w2_ref[...], preferred_element_type=F32) + b2_ref[...]
    o_ref[...] = x_ref[...] + _rms(mix, gpost_ref[...], RMS_EPS)


def _conv(x, u, w_dw, b_dw, ln_g, ln_b, w_pw2, b_pw2, g_post, layer, ic):
    b, s, _ = x.shape
    tiles = s // ROW_TILE
    halo_per_tile = ROW_TILE // HALO_ROWS
    last_halo = s // HALO_ROWS - 1
    row = pl.BlockSpec((None, ROW_TILE, D_MODEL), lambda bi, i: (bi, i, 0))
    prev = pl.BlockSpec((None, HALO_ROWS, D_MODEL),
                        lambda bi, i: (bi, jnp.maximum(i * halo_per_tile - 1, 0), 0))
    nxt = pl.BlockSpec((None, HALO_ROWS, D_MODEL),
                       lambda bi, i: (bi, jnp.minimum((i + 1) * halo_per_tile, last_halo), 0))
    vec = _const_spec((1, D_MODEL), (ic,))
    return pl.pallas_call(
        _conv_kernel,
        grid=(b, tiles),
        in_specs=[row, row, prev, nxt, _const_spec((CONV_WIDTH, D_MODEL), (ic,)), vec, vec, vec,
                  _const_spec((D_MODEL, D_MODEL), (ic,)), vec,
                  _const_spec((1, D_MODEL), (layer,))],
        out_specs=row,
        out_shape=jax.ShapeDtypeStruct((b, s, D_MODEL), F32),
        scratch_shapes=[pltpu.VMEM((ROW_TILE + 2 * HALO_ROWS, D_MODEL), F32),
                        pltpu.VMEM((SUBLANES, SHIFT_ROWS, CONV_COLS), F32),
                        pltpu.VMEM((ROW_TILE, D_MODEL), F32)],
        compiler_params=_params(2),
        name="conv_mix",
    )(x, u, u, u, w_dw, b_dw, ln_g, ln_b, w_pw2, b_pw2, g_post)


def _qkv_kernel(x_ref, g_ref, w_ref, o_ref):
    h = _rms(x_ref[...], g_ref[...], RMS_EPS).astype(BF16)
    q_scale = HEAD_DIM ** -0.5 * LOG2_E
    for part in range(3):
        cols = slice(part * QK_WIDTH, (part + 1) * QK_WIDTH)
        y = jnp.dot(h, w_ref[:, cols], preferred_element_type=F32)
        if part == 0:
            y = y * q_scale
        o_ref[:, cols] = y.astype(BF16)


def _qkv(x, g, w_qkv, layer, ia):
    m = x.shape[0]
    return pl.pallas_call(
        _qkv_kernel,
        grid=(m // ROW_TILE,),
        in_specs=[pl.BlockSpec((ROW_TILE, D_MODEL), lambda i: (i, 0)),
                  _const_spec((1, D_MODEL), (layer,)),
                  _const_spec((D_MODEL, 3 * QK_WIDTH), (ia,))],
        out_specs=pl.BlockSpec((ROW_TILE, 3 * QK_WIDTH), lambda i: (i, 0)),
        out_shape=jax.ShapeDtypeStruct((m, 3 * QK_WIDTH), BF16),
        compiler_params=_params(1),
        name="attn_qkv",
    )(x, g, w_qkv)


def _bias_lanes(lane_rel, pos, coef):
    piece = lane_rel % BIAS_SPLIT
    coef_lane = jnp.where(piece == 0, coef[0], jnp.where(piece == 1, coef[1], coef[2]))
    pos_lo = (pos % POS_RADIX).astype(F32)
    pos_hi = (pos - pos % POS_RADIX).astype(F32)
    pos_lane = jnp.where(lane_rel % (2 * BIAS_SPLIT) < BIAS_SPLIT, pos_lo, pos_hi)
    return pos_lane, coef_lane


def _attn_kernel(slope_ref, lam_ref, subg_ref, q_ref, k_ref, v_ref, o_ref,
                 vt_ref, ka_ref, diag_ref, acc_ref, *, lambda_init, seq):
    qi = pl.program_id(2)
    n_chunks = seq // K_TILE
    coef = slope_ref[:, :1] * LOG2_E
    pieces = []
    rest = coef
    for _ in range(BIAS_SPLIT):
        piece = rest.astype(BF16).astype(F32)
        pieces.append(piece)
        rest = rest - piece

    def side_lane_index(shape, mp):
        lane = lax.broadcasted_iota(jnp.int32, shape, 1)
        lane_rel = lane - (HEAD_DIM if mp == 0 else 0)
        return lane, lane_rel, (lane_rel >= 0) & (lane_rel < 4 * BIAS_SPLIT)

    @pl.when(qi == 0)
    def _per_head_setup():
        ones_row = lax.broadcasted_iota(jnp.int32, (BF16_ROWS, K_TILE), 0) == 0
        for c in range(n_chunks):
            vt_ref[c, :HEAD_WIDTH] = v_ref[c * K_TILE:(c + 1) * K_TILE, :].astype(F32).T.astype(BF16)
            vt_ref[c, HEAD_WIDTH:] = ones_row.astype(BF16)
        jpos = lax.broadcasted_iota(jnp.int32, (K_TILE, HEAD_WIDTH), 0)
        for mp in range(2):
            lane, lane_rel, is_side = side_lane_index((K_TILE, HEAD_WIDTH), mp)
            pos_lane, coef_lane = _bias_lanes(lane_rel, jpos, pieces)
            side = jnp.where(lane_rel < 2 * BIAS_SPLIT, coef_lane, pos_lane)
            side = jnp.where(is_side, side, 0.0).astype(BF16)
            own = (lane < HEAD_DIM) if mp == 0 else (lane >= HEAD_DIM)
            for c in range(n_chunks):
                k = k_ref[c * K_TILE:(c + 1) * K_TILE, :]
                ka_ref[mp, c] = jnp.where(own, k, side)
        rel = (lax.broadcasted_iota(jnp.int32, (K_TILE, Q_TILE), 0)
               - lax.broadcasted_iota(jnp.int32, (K_TILE, Q_TILE), 1))
        diag_ref[...] = jnp.abs(rel).astype(F32) * coef

    q = q_ref[...]
    ipos = lax.broadcasted_iota(jnp.int32, q.shape, 0)
    q_own, q_after, q_before = [], [], []
    for mp in range(2):
        lane, lane_rel, is_side = side_lane_index(q.shape, mp)
        pos_lane, coef_lane = _bias_lanes(lane_rel, ipos, pieces)
        side = jnp.where(lane_rel < 2 * BIAS_SPLIT, pos_lane, -coef_lane)
        side = jnp.where(is_side, side, 0.0)
        own = (lane < HEAD_DIM) if mp == 0 else (lane >= HEAD_DIM)
        q_own.append(jnp.where(own, q, jnp.zeros_like(q)))
        q_after.append(jnp.where(own, q, side.astype(BF16)))
        q_before.append(jnp.where(own, q, (-side).astype(BF16)))

    def scores(t):
        chunk = qi + t
        wrapped = chunk >= n_chunks
        chunk = jnp.where(wrapped, chunk - n_chunks, chunk)
        dist = jnp.where(wrapped, n_chunks - t, t)
        const = coef * (dist * K_TILE).astype(F32)
        out = []
        for mp in range(2):
            if t == 0:
                qm = q_own[mp]
            else:
                qm = jnp.where(wrapped, q_before[mp], q_after[mp])
            s = lax.dot_general(ka_ref[mp, chunk], qm, (((1,), (1,)), ((), ())),
                                preferred_element_type=F32)
            if t == 0:
                s = s - diag_ref[...]
            out.append(s)
        return out, const, chunk

    ahead = scores(0)
    ref_max = [jnp.max(s, axis=0, keepdims=True) for s in ahead[0]]
    acc = [None, None]
    for t in range(n_chunks):
        s_cur, const, chunk = ahead
        if t + 1 < n_chunks:
            ahead = scores(t + 1)
        vt = vt_ref[chunk]
        for mp in range(2):
            p = jnp.exp2(s_cur[mp] - (ref_max[mp] + const)).astype(BF16)
            pv = jnp.dot(vt, p, preferred_element_type=F32)
            acc[mp] = pv if t == 0 else acc[mp] + pv
    acc_ref[0] = acc[0]
    acc_ref[1] = acc[1]

    overflowed = jnp.logical_not((jnp.abs(acc[0]) < F32_HUGE) & (jnp.abs(acc[1]) < F32_HUGE))
    any_overflow = jnp.max(overflowed.astype(F32)) > 0.0

    @pl.when(any_overflow)
    def _exact_softmax():
        kpos = lax.broadcasted_iota(jnp.int32, (K_TILE, Q_TILE), 0)
        qpos = lax.broadcasted_iota(jnp.int32, (K_TILE, Q_TILE), 1) + qi * Q_TILE

        def chunk_scores(c):
            start = pl.multiple_of(c * K_TILE, K_TILE)
            k = k_ref[pl.ds(start, K_TILE), :]
            bias = jnp.abs(kpos + start - qpos).astype(F32) * coef
            return [lax.dot_general(k, qm, (((1,), (1,)), ((), ())),
                                    preferred_element_type=F32) - bias for qm in q_own]

        def max_pass(c, m):
            return tuple(jnp.maximum(mm, jnp.max(s, axis=0, keepdims=True))
                         for mm, s in zip(m, chunk_scores(c)))

        neg_inf = jnp.full((1, Q_TILE), -jnp.inf, F32)
        true_max = lax.fori_loop(0, n_chunks, max_pass, (neg_inf, neg_inf))
        acc_ref[...] = jnp.zeros(acc_ref.shape, F32)

        def exp_pass(c, carry):
            for mp, s in enumerate(chunk_scores(c)):
                p = jnp.exp2(s - true_max[mp]).astype(BF16)
                acc_ref[mp] += jnp.dot(vt_ref[c], p, preferred_element_type=F32)
            return carry

        lax.fori_loop(0, n_chunks, exp_pass, 0)

    acc = [acc_ref[0], acc_ref[1]]

    lq = lam_ref[...]
    lam = (jnp.exp(jnp.sum(lq[0:1] * lq[1:2], axis=-1, keepdims=True))
           - jnp.exp(jnp.sum(lq[2:3] * lq[3:4], axis=-1, keepdims=True)) + lambda_init)
    ot = [a[:HEAD_WIDTH] / a[HEAD_WIDTH:HEAD_WIDTH + 1] for a in acc]
    o = (ot[0] - lam * ot[1]).T
    o = _rms(o, subg_ref[...], SUBLN_EPS) * (1.0 - lambda_init)
    o_ref[...] = o.astype(BF16)


def _attn(qkv, slopes, lam_vecs, subln_g, lambda_init, ia):
    b, s, _ = qkv.shape
    kernel = functools.partial(_attn_kernel, lambda_init=lambda_init, seq=s)
    return pl.pallas_call(
        kernel,
        grid=(b, N_HEADS, s // Q_TILE),
        in_specs=[pl.BlockSpec((None, 1, HEAD_WIDTH), lambda bi, h, i: (h, 0, 0)),
                  _const_spec((4, HEAD_DIM), (ia,)), _const_spec((1, HEAD_WIDTH), (ia,)),
                  pl.BlockSpec((None, Q_TILE, HEAD_WIDTH), lambda bi, h, i: (bi, i, h)),
                  pl.BlockSpec((None, s, HEAD_WIDTH), lambda bi, h, i: (bi, 0, N_HEADS + h)),
                  pl.BlockSpec((None, s, HEAD_WIDTH), lambda bi, h, i: (bi, 0, 2 * N_HEADS + h))],
        out_specs=pl.BlockSpec((None, Q_TILE, HEAD_WIDTH), lambda bi, h, i: (bi, i, h)),
        out_shape=jax.ShapeDtypeStruct((b, s, QK_WIDTH), BF16),
        scratch_shapes=[pltpu.VMEM((s // K_TILE, HEAD_WIDTH + BF16_ROWS, K_TILE), BF16),
                        pltpu.VMEM((2, s // K_TILE, K_TILE, HEAD_WIDTH), BF16),
                        pltpu.VMEM((K_TILE, Q_TILE), F32),
                        pltpu.VMEM((2, HEAD_WIDTH + BF16_ROWS, Q_TILE), F32)],
        compiler_params=_params(3),
        name="attn_core",
    )(slopes, lam_vecs, subln_g, qkv, qkv, qkv)


def _oproj_kernel(x_ref, o_ref, w_ref, g_ref, y_ref):
    mix = jnp.dot(o_ref[...], w_ref[...], preferred_element_type=F32)
    y_ref[...] = x_ref[...] + _rms(mix, g_ref[...], RMS_EPS)


def _oproj(x, o, w_o, g_post, layer, ia):
    m = x.shape[0]
    row = pl.BlockSpec((ROW_TILE, D_MODEL), lambda i: (i, 0))
    return pl.pallas_call(
        _oproj_kernel,
        grid=(m // ROW_TILE,),
        in_specs=[row, row, _const_spec((QK_WIDTH, D_MODEL), (ia,)),
                  _const_spec((1, D_MODEL), (layer,))],
        out_specs=row,
        out_shape=jax.ShapeDtypeStruct((m, D_MODEL), F32),
        compiler_params=_params(1),
        name="attn_oproj",
    )(x, o, w_o, g_post)


def _lambda_init(layer_idx):
    return 0.8 - 0.6 * math.exp(-0.3 * layer_idx)


def kernel(x, ffn_norm_pre, ffn_norm_post, ffn_w_gate, ffn_w_up, ffn_w_down, mix_norm_pre, mix_norm_post, conv_w_pw1, conv_b_pw1, conv_w_dw, conv_b_dw, conv_ln_g, conv_ln_b, conv_w_pw2, conv_b_pw2, attn_w_qkv, attn_w_o, attn_lambda_q1, attn_lambda_k1, attn_lambda_q2, attn_lambda_k2, attn_subln_g):
    b, s, d = x.shape
    depth = ffn_norm_pre.shape[0]
    m = b * s
    assert d == D_MODEL and m % ROW_TILE == 0 and s % ROW_TILE == 0
    assert s % Q_TILE == 0 and s % K_TILE == 0

    w_gate, w_up, w_down = (w.astype(BF16) for w in (ffn_w_gate, ffn_w_up, ffn_w_down))
    w_pw1, w_pw2 = conv_w_pw1.astype(BF16), conv_w_pw2.astype(BF16)
    w_qkv, w_o = attn_w_qkv.astype(BF16), attn_w_o.astype(BF16)
    g_ffn_pre, g_ffn_post = ffn_norm_pre[:, :, None, :], ffn_norm_post[:, :, None, :]
    g_mix_pre, g_mix_post = mix_norm_pre[:, None, :], mix_norm_post[:, None, :]
    b_pw1, b_dw, b_pw2 = conv_b_pw1[:, None, :], conv_b_dw[:, None, :], conv_b_pw2[:, None, :]
    ln_g, ln_b = conv_ln_g[:, None, :], conv_ln_b[:, None, :]
    subln_g = attn_subln_g[:, None, :]
    lam_vecs = jnp.stack([attn_lambda_q1, attn_lambda_k1, attn_lambda_q2, attn_lambda_k2], axis=1)
    slopes = 2.0 ** (-8.0 * (jnp.arange(N_HEADS, dtype=F32) + 1.0) / N_HEADS)
    slopes = jnp.broadcast_to(slopes[:, None, None], (N_HEADS, 1, HEAD_WIDTH))

    x = x.reshape(m, d)
    ic = ia = 0
    for i in range(depth):
        x = _ffn(x, g_ffn_pre, g_ffn_post, w_gate, w_up, w_down, (i, 0))
        if i % 2 == 0:
            u = _glu(x, g_mix_pre, w_pw1, b_pw1, i, ic)
            x = _conv(x.reshape(b, s, d), u.reshape(b, s, d), conv_w_dw, b_dw, ln_g, ln_b,
                      w_pw2, b_pw2, g_mix_post, i, ic).reshape(m, d)
            ic += 1
        else:
            qkv = _qkv(x, g_mix_pre, w_qkv, i, ia)
            o = _attn(qkv.reshape(b, s, 3 * QK_WIDTH), slopes, lam_vecs, subln_g,
                      _lambda_init(i), ia)
            x = _oproj(x, o.reshape(m, QK_WIDTH), w_o, g_mix_post, i, ia)
            ia += 1
        x = _ffn(x, g_ffn_pre, g_ffn_post, w_gate, w_up, w_down, (i, 1))
    return x.reshape(b, s, d)
```

```python
import functools
import math

import jax
import jax.numpy as jnp
from jax import lax
from jax.experimental import pallas as pl
from jax.experimental.pallas import tpu as pltpu

D_MODEL = 1024
D_FF = 2816
CONV_WIDTH = 31
CONV_PAD = (CONV_WIDTH - 1) // 2
N_HEADS = 8
HEAD_DIM = 64
HEAD_WIDTH = 2 * HEAD_DIM
QK_WIDTH = N_HEADS * HEAD_WIDTH
RMS_EPS = 1e-6
SUBLN_EPS = 1e-5
LN_EPS = 1e-5
LOG2_E = math.log2(math.e)
F32_HUGE = 3.0e38

V7X_VMEM_BYTES = 64 * 1024 * 1024
VMEM_LIMIT_BYTES = V7X_VMEM_BYTES * 7 // 8

ROW_TILE = 512
FFN_ROW_TILE = 1024
FFN_SUB_ROWS = 512
FF_CHUNK = 256
HALO_ROWS = 16
CONV_ROWS = 32
CONV_COLS = 256
SUBLANES = 8
BF16_ROWS = 16
SHIFT_ROWS = ROW_TILE + 2 * HALO_ROWS - SUBLANES
Q_TILE = 512
K_TILE = 512
TILES_PER_STEP = 2
BIAS_SPLIT = 3
POS_RADIX = 256
assert Q_TILE == K_TILE and K_TILE <= POS_RADIX * POS_RADIX

F32 = jnp.float32
BF16 = jnp.bfloat16


def _rms(x, g, eps):
    return x * lax.rsqrt(jnp.mean(x * x, axis=-1, keepdims=True) + eps) * g


def _const_spec(shape, lead=()):
    zeros = (0,) * len(shape)
    return pl.BlockSpec((None,) * len(lead) + tuple(shape), lambda *_: tuple(lead) + zeros,
                        pipeline_mode=pl.Buffered(1))


def _params(n_axes):
    return pltpu.CompilerParams(dimension_semantics=("arbitrary",) * n_axes,
                                vmem_limit_bytes=VMEM_LIMIT_BYTES)


def _sub_blocks():
    return [slice(r * FFN_SUB_ROWS, (r + 1) * FFN_SUB_ROWS)
            for r in range(FFN_ROW_TILE // FFN_SUB_ROWS)]


def _ffn_kernel(x_ref, gpre_ref, gpost_ref, wg_ref, wu_ref, wd_ref, o_ref, p_ref):
    for r, rows in enumerate(_sub_blocks()):
        x = x_ref[rows, :]
        h = _rms(x, gpre_ref[...], RMS_EPS).astype(BF16)
        for c in range(D_FF // FF_CHUNK):
            cols = slice(c * FF_CHUNK, (c + 1) * FF_CHUNK)
            g = jnp.dot(h, wg_ref[:, cols], preferred_element_type=F32)
            u = jnp.dot(h, wu_ref[:, cols], preferred_element_type=F32)
            p_ref[r, :, cols] = (g * jax.nn.sigmoid(g) * u).astype(BF16)
        f = jnp.dot(p_ref[r], wd_ref[...], preferred_element_type=F32)
        o_ref[rows, :] = x + 0.5 * _rms(f, gpost_ref[...], RMS_EPS)


def _ffn(x, g_pre, g_post, w_gate, w_up, w_down, idx):
    m = x.shape[0]
    row = pl.BlockSpec((FFN_ROW_TILE, D_MODEL), lambda i: (i, 0))
    return pl.pallas_call(
        _ffn_kernel,
        grid=(m // FFN_ROW_TILE,),
        in_specs=[row, _const_spec((1, D_MODEL), idx), _const_spec((1, D_MODEL), idx),
                  _const_spec((D_MODEL, D_FF), idx), _const_spec((D_MODEL, D_FF), idx),
                  _const_spec((D_FF, D_MODEL), idx)],
        out_specs=row,
        out_shape=jax.ShapeDtypeStruct((m, D_MODEL), F32),
        scratch_shapes=[pltpu.VMEM((FFN_ROW_TILE // FFN_SUB_ROWS, FFN_SUB_ROWS, D_FF), BF16)],
        compiler_params=_params(1),
        name="ffn",
    )(x, g_pre, g_post, w_gate, w_up, w_down)


def _glu_kernel(x_ref, g_ref, w_ref, b_ref, u_ref):
    for rows in _sub_blocks():
        h = _rms(x_ref[rows, :], g_ref[...], RMS_EPS).astype(BF16)
        a = jnp.dot(h, w_ref[:, :D_MODEL], preferred_element_type=F32) + b_ref[:, :D_MODEL]
        t = jnp.dot(h, w_ref[:, D_MODEL:], preferred_element_type=F32) + b_ref[:, D_MODEL:]
        u_ref[rows, :] = a * jax.nn.sigmoid(t)


def _glu(x, g, w_pw1, b_pw1, layer, ic):
    m = x.shape[0]
    row = pl.BlockSpec((FFN_ROW_TILE, D_MODEL), lambda i: (i, 0))
    return pl.pallas_call(
        _glu_kernel,
        grid=(m // FFN_ROW_TILE,),
        in_specs=[row, _const_spec((1, D_MODEL), (layer,)),
                  _const_spec((D_MODEL, 2 * D_MODEL), (ic,)),
                  _const_spec((1, 2 * D_MODEL), (ic,))],
        out_specs=row,
        out_shape=jax.ShapeDtypeStruct((m, D_MODEL), F32),
        compiler_params=_params(1),
        name="conv_glu",
    )(x, g, w_pw1, b_pw1)


def _conv_kernel(x_ref, u_ref, prev_ref, next_ref, wdw_ref, bdw_ref, lng_ref, lnb_ref,
                 w2_ref, b2_ref, gpost_ref, o_ref, ubuf_ref, sh_ref, c_ref):
    i = pl.program_id(1)
    n = pl.num_programs(1)
    ubuf_ref[:HALO_ROWS, :] = jnp.where(i > 0, prev_ref[...], 0.0)
    ubuf_ref[HALO_ROWS:HALO_ROWS + ROW_TILE, :] = u_ref[...]
    ubuf_ref[HALO_ROWS + ROW_TILE:, :] = jnp.where(i < n - 1, next_ref[...], 0.0)

    for cb in range(D_MODEL // CONV_COLS):
        cols = slice(cb * CONV_COLS, (cb + 1) * CONV_COLS)
        for phase in range(SUBLANES):
            sh_ref[phase] = ubuf_ref[phase:phase + SHIFT_ROWS, cols]

        def row_block(r, carry, cols=cols):
            base = pl.multiple_of(r * CONV_ROWS, CONV_ROWS)
            acc = jnp.zeros((CONV_ROWS, CONV_COLS), F32)
            for k in range(CONV_WIDTH):
                step, phase = divmod(HALO_ROWS - CONV_PAD + k, SUBLANES)
                win = sh_ref[phase, pl.ds(base + step * SUBLANES, CONV_ROWS), :]
                acc = acc + win * wdw_ref[k:k + 1, cols]
            c_ref[pl.ds(base, CONV_ROWS), cols] = acc
            return carry

        lax.fori_loop(0, ROW_TILE // CONV_ROWS, row_block, 0)

    c = c_ref[...] + bdw_ref[...]
    mu = jnp.mean(c, axis=-1, keepdims=True)
    d = c - mu
    var = jnp.mean(d * d, axis=-1, keepdims=True)
    y = d * lax.rsqrt(var + LN_EPS) * lng_ref[...] + lnb_ref[...]
    y = (y * jax.nn.sigmoid(y)).astype(BF16)
    mix = jnp.dot(y, w2_ref[...], preferred_element_type=F32) + b2_ref[...]
    o_ref[...] = x_ref[...] + _rms(mix, gpost_ref[...], RMS_EPS)


def _conv(x, u, w_dw, b_dw, ln_g, ln_b, w_pw2, b_pw2, g_post, layer, ic):
    b, s, _ = x.shape
    tiles = s // ROW_TILE
    halo_per_tile = ROW_TILE // HALO_ROWS
    last_halo = s // HALO_ROWS - 1
    row = pl.BlockSpec((None, ROW_TILE, D_MODEL), lambda bi, i: (bi, i, 0))
    prev = pl.BlockSpec((None, HALO_ROWS, D_MODEL),
                        lambda bi, i: (bi, jnp.maximum(i * halo_per_tile - 1, 0), 0))
    nxt = pl.BlockSpec((None, HALO_ROWS, D_MODEL),
                       lambda bi, i: (bi, jnp.minimum((i + 1) * halo_per_tile, last_halo), 0))
    vec = _const_spec((1, D_MODEL), (ic,))
    return pl.pallas_call(
        _conv_kernel,
        grid=(b, tiles),
        in_specs=[row, row, prev, nxt, _const_spec((CONV_WIDTH, D_MODEL), (ic,)), vec, vec, vec,
                  _const_spec((D_MODEL, D_MODEL), (ic,)), vec,
                  _const_spec((1, D_MODEL), (layer,))],
        out_specs=row,
        out_shape=jax.ShapeDtypeStruct((b, s, D_MODEL), F32),
        scratch_shapes=[pltpu.VMEM((ROW_TILE + 2 * HALO_ROWS, D_MODEL), F32),
                        pltpu.VMEM((SUBLANES, SHIFT_ROWS, CONV_COLS), F32),
                        pltpu.VMEM((ROW_TILE, D_MODEL), F32)],
        compiler_params=_params(2),
        name="conv_mix",
    )(x, u, u, u, w_dw, b_dw, ln_g, ln_b, w_pw2, b_pw2, g_post)


def _qkv_kernel(x_ref, g_ref, w_ref, o_ref):
    q_scale = HEAD_DIM ** -0.5 * LOG2_E
    for rows in _sub_blocks():
        h = _rms(x_ref[rows, :], g_ref[...], RMS_EPS).astype(BF16)
        for part in range(3):
            cols = slice(part * QK_WIDTH, (part + 1) * QK_WIDTH)
            y = jnp.dot(h, w_ref[:, cols], preferred_element_type=F32)
            if part == 0:
                y = y * q_scale
            o_ref[rows, cols] = y.astype(BF16)


def _qkv(x, g, w_qkv, layer, ia):
    m = x.shape[0]
    return pl.pallas_call(
        _qkv_kernel,
        grid=(m // FFN_ROW_TILE,),
        in_specs=[pl.BlockSpec((FFN_ROW_TILE, D_MODEL), lambda i: (i, 0)),
                  _const_spec((1, D_MODEL), (layer,)),
                  _const_spec((D_MODEL, 3 * QK_WIDTH), (ia,))],
        out_specs=pl.BlockSpec((FFN_ROW_TILE, 3 * QK_WIDTH), lambda i: (i, 0)),
        out_shape=jax.ShapeDtypeStruct((m, 3 * QK_WIDTH), BF16),
        compiler_params=_params(1),
        name="attn_qkv",
    )(x, g, w_qkv)


def _bias_lanes(lane_rel, pos, coef):
    piece = lane_rel % BIAS_SPLIT
    coef_lane = jnp.where(piece == 0, coef[0], jnp.where(piece == 1, coef[1], coef[2]))
    pos_lo = (pos % POS_RADIX).astype(F32)
    pos_hi = (pos - pos % POS_RADIX).astype(F32)
    pos_lane = jnp.where(lane_rel % (2 * BIAS_SPLIT) < BIAS_SPLIT, pos_lo, pos_hi)
    return pos_lane, coef_lane


def _attn_kernel(slope_ref, lam_ref, subg_ref, q_ref, k_ref, v_ref, o_ref,
                 vt_ref, ka_ref, diag_ref, acc_ref, *, lambda_init, seq):
    step = pl.program_id(2)
    n_chunks = seq // K_TILE
    coef = slope_ref[:, :1] * LOG2_E
    pieces = []
    rest = coef
    for _ in range(BIAS_SPLIT):
        piece = rest.astype(BF16).astype(F32)
        pieces.append(piece)
        rest = rest - piece

    def side_lane_index(shape, mp):
        lane = lax.broadcasted_iota(jnp.int32, shape, 1)
        lane_rel = lane - (HEAD_DIM if mp == 0 else 0)
        return lane, lane_rel, (lane_rel >= 0) & (lane_rel < 4 * BIAS_SPLIT)

    @pl.when(step == 0)
    def _per_head_setup():
        ones_row = lax.broadcasted_iota(jnp.int32, (BF16_ROWS, K_TILE), 0) == 0
        for c in range(n_chunks):
            vt_ref[c, :HEAD_WIDTH] = v_ref[c * K_TILE:(c + 1) * K_TILE, :].astype(F32).T.astype(BF16)
            vt_ref[c, HEAD_WIDTH:] = ones_row.astype(BF16)
        jpos = lax.broadcasted_iota(jnp.int32, (K_TILE, HEAD_WIDTH), 0)
        for mp in range(2):
            lane, lane_rel, is_side = side_lane_index((K_TILE, HEAD_WIDTH), mp)
            pos_lane, coef_lane = _bias_lanes(lane_rel, jpos, pieces)
            side = jnp.where(lane_rel < 2 * BIAS_SPLIT, coef_lane, pos_lane)
            side = jnp.where(is_side, side, 0.0).astype(BF16)
            own = (lane < HEAD_DIM) if mp == 0 else (lane >= HEAD_DIM)
            for c in range(n_chunks):
                k = k_ref[c * K_TILE:(c + 1) * K_TILE, :]
                ka_ref[mp, c] = jnp.where(own, k, side)
        rel = (lax.broadcasted_iota(jnp.int32, (K_TILE, Q_TILE), 0)
               - lax.broadcasted_iota(jnp.int32, (K_TILE, Q_TILE), 1))
        diag_ref[...] = jnp.abs(rel).astype(F32) * coef

    def query_operands(q):
        ipos = lax.broadcasted_iota(jnp.int32, q.shape, 0)
        q_own, q_after, q_before = [], [], []
        for mp in range(2):
            lane, lane_rel, is_side = side_lane_index(q.shape, mp)
            pos_lane, coef_lane = _bias_lanes(lane_rel, ipos, pieces)
            side = jnp.where(lane_rel < 2 * BIAS_SPLIT, pos_lane, -coef_lane)
            side = jnp.where(is_side, side, 0.0)
            own = (lane < HEAD_DIM) if mp == 0 else (lane >= HEAD_DIM)
            q_own.append(jnp.where(own, q, jnp.zeros_like(q)))
            q_after.append(jnp.where(own, q, side.astype(BF16)))
            q_before.append(jnp.where(own, q, (-side).astype(BF16)))
        return q_own, q_after, q_before

    def fast_tile(tile, q):
        q_own, q_after, q_before = query_operands(q)

        def scores(t):
            chunk = tile + t
            wrapped = chunk >= n_chunks
            chunk = jnp.where(wrapped, chunk - n_chunks, chunk)
            dist = jnp.where(wrapped, n_chunks - t, t)
            const = coef * (dist * K_TILE).astype(F32)
            out = []
            for mp in range(2):
                if t == 0:
                    qm = q_own[mp]
                else:
                    qm = jnp.where(wrapped, q_before[mp], q_after[mp])
                s = lax.dot_general(ka_ref[mp, chunk], qm, (((1,), (1,)), ((), ())),
                                    preferred_element_type=F32)
                if t == 0:
                    s = s - diag_ref[...]
                out.append(s)
            return out, const, chunk

        ahead = scores(0)
        ref_max = [jnp.max(s, axis=0, keepdims=True) for s in ahead[0]]
        acc = [None, None]
        for t in range(n_chunks):
            s_cur, const, chunk = ahead
            if t + 1 < n_chunks:
                ahead = scores(t + 1)
            vt = vt_ref[chunk]
            for mp in range(2):
                p = jnp.exp2(s_cur[mp] - (ref_max[mp] + const)).astype(BF16)
                pv = jnp.dot(vt, p, preferred_element_type=F32)
                acc[mp] = pv if t == 0 else acc[mp] + pv
        return acc

    def exact_tile(tile, q):
        q_own = query_operands(q)[0]
        kpos = lax.broadcasted_iota(jnp.int32, (K_TILE, Q_TILE), 0)
        qpos = lax.broadcasted_iota(jnp.int32, (K_TILE, Q_TILE), 1) + tile * Q_TILE

        def chunk_scores(c):
            start = pl.multiple_of(c * K_TILE, K_TILE)
            k = k_ref[pl.ds(start, K_TILE), :]
            bias = jnp.abs(kpos + start - qpos).astype(F32) * coef
            return [lax.dot_general(k, qm, (((1,), (1,)), ((), ())),
                                    preferred_element_type=F32) - bias for qm in q_own]

        def max_pass(c, m):
            return tuple(jnp.maximum(mm, jnp.max(s, axis=0, keepdims=True))
                         for mm, s in zip(m, chunk_scores(c)))

        neg_inf = jnp.full((1, Q_TILE), -jnp.inf, F32)
        true_max = lax.fori_loop(0, n_chunks, max_pass, (neg_inf, neg_inf))
        acc_ref[...] = jnp.zeros(acc_ref.shape, F32)

        def exp_pass(c, carry):
            for mp, s in enumerate(chunk_scores(c)):
                p = jnp.exp2(s - true_max[mp]).astype(BF16)
                acc_ref[mp] += jnp.dot(vt_ref[c], p, preferred_element_type=F32)
            return carry

        lax.fori_loop(0, n_chunks, exp_pass, 0)
        return [acc_ref[0], acc_ref[1]]

    def finish(acc, rows):
        lq = lam_ref[...]
        lam = (jnp.exp(jnp.sum(lq[0:1] * lq[1:2], axis=-1, keepdims=True))
               - jnp.exp(jnp.sum(lq[2:3] * lq[3:4], axis=-1, keepdims=True)) + lambda_init)
        ot = [a[:HEAD_WIDTH] / a[HEAD_WIDTH:HEAD_WIDTH + 1] for a in acc]
        o = (ot[0] - lam * ot[1]).T
        o = _rms(o, subg_ref[...], SUBLN_EPS) * (1.0 - lambda_init)
        o_ref[rows, :] = o.astype(BF16)

    overflow = []
    for j in range(TILES_PER_STEP):
        rows = slice(j * Q_TILE, (j + 1) * Q_TILE)
        acc = fast_tile(step * TILES_PER_STEP + j, q_ref[rows, :])
        finish(acc, rows)
        bad = jnp.logical_not((jnp.abs(acc[0]) < F32_HUGE) & (jnp.abs(acc[1]) < F32_HUGE))
        overflow.append(jnp.max(bad.astype(F32)) > 0.0)

    for j in range(TILES_PER_STEP):
        rows = slice(j * Q_TILE, (j + 1) * Q_TILE)

        @pl.when(overflow[j])
        def _redo_exactly(j=j, rows=rows):
            finish(exact_tile(step * TILES_PER_STEP + j, q_ref[rows, :]), rows)


def _attn(qkv, slopes, lam_vecs, subln_g, lambda_init, ia):
    b, s, _ = qkv.shape
    kernel = functools.partial(_attn_kernel, lambda_init=lambda_init, seq=s)
    step_rows = TILES_PER_STEP * Q_TILE
    return pl.pallas_call(
        kernel,
        grid=(b, N_HEADS, s // step_rows),
        in_specs=[pl.BlockSpec((None, 1, HEAD_WIDTH), lambda bi, h, i: (h, 0, 0)),
                  _const_spec((4, HEAD_DIM), (ia,)), _const_spec((1, HEAD_WIDTH), (ia,)),
                  pl.BlockSpec((None, step_rows, HEAD_WIDTH), lambda bi, h, i: (bi, i, h)),
                  pl.BlockSpec((None, s, HEAD_WIDTH), lambda bi, h, i: (bi, 0, N_HEADS + h)),
                  pl.BlockSpec((None, s, HEAD_WIDTH), lambda bi, h, i: (bi, 0, 2 * N_HEADS + h))],
        out_specs=pl.BlockSpec((None, step_rows, HEAD_WIDTH), lambda bi, h, i: (bi, i, h)),
        out_shape=jax.ShapeDtypeStruct((b, s, QK_WIDTH), BF16),
        scratch_shapes=[pltpu.VMEM((s // K_TILE, HEAD_WIDTH + BF16_ROWS, K_TILE), BF16),
                        pltpu.VMEM((2, s // K_TILE, K_TILE, HEAD_WIDTH), BF16),
                        pltpu.VMEM((K_TILE, Q_TILE), F32),
                        pltpu.VMEM((2, HEAD_WIDTH + BF16_ROWS, Q_TILE), F32)],
        compiler_params=_params(3),
        name="attn_core",
    )(slopes, lam_vecs, subln_g, qkv, qkv, qkv)


def _oproj_kernel(x_ref, o_ref, w_ref, g_ref, y_ref):
    for rows in _sub_blocks():
        mix = jnp.dot(o_ref[rows, :], w_ref[...], preferred_element_type=F32)
        y_ref[rows, :] = x_ref[rows, :] + _rms(mix, g_ref[...], RMS_EPS)


def _oproj(x, o, w_o, g_post, layer, ia):
    m = x.shape[0]
    row = pl.BlockSpec((FFN_ROW_TILE, D_MODEL), lambda i: (i, 0))
    return pl.pallas_call(
        _oproj_kernel,
        grid=(m // FFN_ROW_TILE,),
        in_specs=[row, row, _const_spec((QK_WIDTH, D_MODEL), (ia,)),
                  _const_spec((1, D_MODEL), (layer,))],
        out_specs=row,
        out_shape=jax.ShapeDtypeStruct((m, D_MODEL), F32),
        compiler_params=_params(1),
        name="attn_oproj",
    )(x, o, w_o, g_post)


def _lambda_init(layer_idx):
    return 0.8 - 0.6 * math.exp(-0.3 * layer_idx)


def kernel(x, ffn_norm_pre, ffn_norm_post, ffn_w_gate, ffn_w_up, ffn_w_down, mix_norm_pre, mix_norm_post, conv_w_pw1, conv_b_pw1, conv_w_dw, conv_b_dw, conv_ln_g, conv_ln_b, conv_w_pw2, conv_b_pw2, attn_w_qkv, attn_w_o, attn_lambda_q1, attn_lambda_k1, attn_lambda_q2, attn_lambda_k2, attn_subln_g):
    b, s, d = x.shape
    depth = ffn_norm_pre.shape[0]
    m = b * s
    assert d == D_MODEL and m % ROW_TILE == 0 and s % ROW_TILE == 0 and m % FFN_ROW_TILE == 0
    assert s % (TILES_PER_STEP * Q_TILE) == 0 and s % K_TILE == 0

    w_gate, w_up, w_down = (w.astype(BF16) for w in (ffn_w_gate, ffn_w_up, ffn_w_down))
    w_pw1, w_pw2 = conv_w_pw1.astype(BF16), conv_w_pw2.astype(BF16)
    w_qkv, w_o = attn_w_qkv.astype(BF16), attn_w_o.astype(BF16)
    g_ffn_pre, g_ffn_post = ffn_norm_pre[:, :, None, :], ffn_norm_post[:, :, None, :]
    g_mix_pre, g_mix_post = mix_norm_pre[:, None, :], mix_norm_post[:, None, :]
    b_pw1, b_dw, b_pw2 = conv_b_pw1[:, None, :], conv_b_dw[:, None, :], conv_b_pw2[:, None, :]
    ln_g, ln_b = conv_ln_g[:, None, :], conv_ln_b[:, None, :]
    subln_g = attn_subln_g[:, None, :]
    lam_vecs = jnp.stack([attn_lambda_q1, attn_lambda_k1, attn_lambda_q2, attn_lambda_k2], axis=1)
    slopes = 2.0 ** (-8.0 * (jnp.arange(N_HEADS, dtype=F32) + 1.0) / N_HEADS)
    slopes = jnp.broadcast_to(slopes[:, None, None], (N_HEADS, 1, HEAD_WIDTH))

    x = x.reshape(m, d)
    ic = ia = 0
    for i in range(depth):
        x = _ffn(x, g_ffn_pre, g_ffn_post, w_gate, w_up, w_down, (i, 0))
        if i % 2 == 0:
            u = _glu(x, g_mix_pre, w_pw1, b_pw1, i, ic)
            x = _conv(x.reshape(b, s, d), u.reshape(b, s, d), conv_w_dw, b_dw, ln_g, ln_b,
                      w_pw2, b_pw2, g_mix_post, i, ic).reshape(m, d)
            ic += 1
        else:
            qkv = _qkv(x, g_mix_pre, w_qkv, i, ia)
            o = _attn(qkv.reshape(b, s, 3 * QK_WIDTH), slopes, lam_vecs, subln_g,
                      _lambda_init(i), ia)
            x = _oproj(x, o.reshape(m, QK_WIDTH), w_o, g_mix_post, i, ia)
            ia += 1
        x = _ffn(x, g_ffn_pre, g_ffn_post, w_gate, w_up, w_down, (i, 1))
    return x.reshape(b, s, d)
```

```python
import functools
import math

import jax
import jax.numpy as jnp
from jax import lax
from jax.experimental import pallas as pl
from jax.experimental.pallas import tpu as pltpu

D_MODEL = 1024
D_FF = 2816
CONV_WIDTH = 31
CONV_PAD = (CONV_WIDTH - 1) // 2
N_HEADS = 8
HEAD_DIM = 64
HEAD_WIDTH = 2 * HEAD_DIM
QK_WIDTH = N_HEADS * HEAD_WIDTH
RMS_EPS = 1e-6
SUBLN_EPS = 1e-5
LN_EPS = 1e-5
LOG2_E = math.log2(math.e)
F32_HUGE = 3.0e38

V7X_VMEM_BYTES = 64 * 1024 * 1024
VMEM_LIMIT_BYTES = V7X_VMEM_BYTES * 7 // 8

ROW_TILE = 512
FFN_ROW_TILE = 1024
FFN_SUB_ROWS = 512
FF_CHUNK = 256
HALO_ROWS = 16
CONV_ROWS = 32
CONV_COLS = 256
SUBLANES = 8
BF16_ROWS = 16
SHIFT_ROWS = ROW_TILE + 2 * HALO_ROWS - SUBLANES
Q_TILE = 512
K_TILE = 512
TILES_PER_STEP = 2
VISIT_RADII = (1, 2)
EXP2_UNDERFLOW = 150.0
NORM_SLACK = 1.01
BIAS_SPLIT = 3
POS_RADIX = 256
assert Q_TILE == K_TILE and K_TILE <= POS_RADIX * POS_RADIX

F32 = jnp.float32
BF16 = jnp.bfloat16


def _rms(x, g, eps):
    return x * lax.rsqrt(jnp.mean(x * x, axis=-1, keepdims=True) + eps) * g


def _const_spec(shape, lead=()):
    zeros = (0,) * len(shape)
    return pl.BlockSpec((None,) * len(lead) + tuple(shape), lambda *_: tuple(lead) + zeros,
                        pipeline_mode=pl.Buffered(1))


def _params(n_axes):
    return pltpu.CompilerParams(dimension_semantics=("arbitrary",) * n_axes,
                                vmem_limit_bytes=VMEM_LIMIT_BYTES)


def _sub_blocks():
    return [slice(r * FFN_SUB_ROWS, (r + 1) * FFN_SUB_ROWS)
            for r in range(FFN_ROW_TILE // FFN_SUB_ROWS)]


def _ffn_kernel(x_ref, gpre_ref, gpost_ref, wg_ref, wu_ref, wd_ref, o_ref, p_ref):
    for r, rows in enumerate(_sub_blocks()):
        x = x_ref[rows, :]
        h = _rms(x, gpre_ref[...], RMS_EPS).astype(BF16)
        for c in range(D_FF // FF_CHUNK):
            cols = slice(c * FF_CHUNK, (c + 1) * FF_CHUNK)
            g = jnp.dot(h, wg_ref[:, cols], preferred_element_type=F32)
            u = jnp.dot(h, wu_ref[:, cols], preferred_element_type=F32)
            p_ref[r, :, cols] = (g * jax.nn.sigmoid(g) * u).astype(BF16)
        f = jnp.dot(p_ref[r], wd_ref[...], preferred_element_type=F32)
        o_ref[rows, :] = x + 0.5 * _rms(f, gpost_ref[...], RMS_EPS)


def _ffn(x, g_pre, g_post, w_gate, w_up, w_down, idx):
    m = x.shape[0]
    row = pl.BlockSpec((FFN_ROW_TILE, D_MODEL), lambda i: (i, 0))
    return pl.pallas_call(
        _ffn_kernel,
        grid=(m // FFN_ROW_TILE,),
        in_specs=[row, _const_spec((1, D_MODEL), idx), _const_spec((1, D_MODEL), idx),
                  _const_spec((D_MODEL, D_FF), idx), _const_spec((D_MODEL, D_FF), idx),
                  _const_spec((D_FF, D_MODEL), idx)],
        out_specs=row,
        out_shape=jax.ShapeDtypeStruct((m, D_MODEL), F32),
        scratch_shapes=[pltpu.VMEM((FFN_ROW_TILE // FFN_SUB_ROWS, FFN_SUB_ROWS, D_FF), BF16)],
        compiler_params=_params(1),
        name="ffn",
    )(x, g_pre, g_post, w_gate, w_up, w_down)


def _glu_kernel(x_ref, g_ref, w_ref, b_ref, u_ref):
    for rows in _sub_blocks():
        h = _rms(x_ref[rows, :], g_ref[...], RMS_EPS).astype(BF16)
        a = jnp.dot(h, w_ref[:, :D_MODEL], preferred_element_type=F32) + b_ref[:, :D_MODEL]
        t = jnp.dot(h, w_ref[:, D_MODEL:], preferred_element_type=F32) + b_ref[:, D_MODEL:]
        u_ref[rows, :] = a * jax.nn.sigmoid(t)


def _glu(x, g, w_pw1, b_pw1, layer, ic):
    m = x.shape[0]
    row = pl.BlockSpec((FFN_ROW_TILE, D_MODEL), lambda i: (i, 0))
    return pl.pallas_call(
        _glu_kernel,
        grid=(m // FFN_ROW_TILE,),
        in_specs=[row, _const_spec((1, D_MODEL), (layer,)),
                  _const_spec((D_MODEL, 2 * D_MODEL), (ic,)),
                  _const_spec((1, 2 * D_MODEL), (ic,))],
        out_specs=row,
        out_shape=jax.ShapeDtypeStruct((m, D_MODEL), F32),
        compiler_params=_params(1),
        name="conv_glu",
    )(x, g, w_pw1, b_pw1)


def _conv_kernel(x_ref, u_ref, prev_ref, next_ref, wdw_ref, bdw_ref, lng_ref, lnb_ref,
                 w2_ref, b2_ref, gpost_ref, o_ref, ubuf_ref, sh_ref, c_ref):
    i = pl.program_id(1)
    n = pl.num_programs(1)
    ubuf_ref[:HALO_ROWS, :] = jnp.where(i > 0, prev_ref[...], 0.0)
    ubuf_ref[HALO_ROWS:HALO_ROWS + ROW_TILE, :] = u_ref[...]
    ubuf_ref[HALO_ROWS + ROW_TILE:, :] = jnp.where(i < n - 1, next_ref[...], 0.0)

    for cb in range(D_MODEL // CONV_COLS):
        cols = slice(cb * CONV_COLS, (cb + 1) * CONV_COLS)
        for phase in range(SUBLANES):
            sh_ref[phase] = ubuf_ref[phase:phase + SHIFT_ROWS, cols]

        def row_block(r, carry, cols=cols):
            base = pl.multiple_of(r * CONV_ROWS, CONV_ROWS)
            acc = jnp.zeros((CONV_ROWS, CONV_COLS), F32)
            for k in range(CONV_WIDTH):
                step, phase = divmod(HALO_ROWS - CONV_PAD + k, SUBLANES)
                win = sh_ref[phase, pl.ds(base + step * SUBLANES, CONV_ROWS), :]
                acc = acc + win * wdw_ref[k:k + 1, cols]
            c_ref[pl.ds(base, CONV_ROWS), cols] = acc
            return carry

        lax.fori_loop(0, ROW_TILE // CONV_ROWS, row_block, 0)

    c = c_ref[...] + bdw_ref[...]
    mu = jnp.mean(c, axis=-1, keepdims=True)
    d = c - mu
    var = jnp.mean(d * d, axis=-1, keepdims=True)
    y = d * lax.rsqrt(var + LN_EPS) * lng_ref[...] + lnb_ref[...]
    y = (y * jax.nn.sigmoid(y)).astype(BF16)
    mix = jnp.dot(y, w2_ref[...], preferred_element_type=F32) + b2_ref[...]
    o_ref[...] = x_ref[...] + _rms(mix, gpost_ref[...], RMS_EPS)


def _conv(x, u, w_dw, b_dw, ln_g, ln_b, w_pw2, b_pw2, g_post, layer, ic):
    b, s, _ = x.shape
    tiles = s // ROW_TILE
    halo_per_tile = ROW_TILE // HALO_ROWS
    last_halo = s // HALO_ROWS - 1
    row = pl.BlockSpec((None, ROW_TILE, D_MODEL), lambda bi, i: (bi, i, 0))
    prev = pl.BlockSpec((None, HALO_ROWS, D_MODEL),
                        lambda bi, i: (bi, jnp.maximum(i * halo_per_tile - 1, 0), 0))
    nxt = pl.BlockSpec((None, HALO_ROWS, D_MODEL),
                       lambda bi, i: (bi, jnp.minimum((i + 1) * halo_per_tile, last_halo), 0))
    vec = _const_spec((1, D_MODEL), (ic,))
    return pl.pallas_call(
        _conv_kernel,
        grid=(b, tiles),
        in_specs=[row, row, prev, nxt, _const_spec((CONV_WIDTH, D_MODEL), (ic,)), vec, vec, vec,
                  _const_spec((D_MODEL, D_MODEL), (ic,)), vec,
                  _const_spec((1, D_MODEL), (layer,))],
        out_specs=row,
        out_shape=jax.ShapeDtypeStruct((b, s, D_MODEL), F32),
        scratch_shapes=[pltpu.VMEM((ROW_TILE + 2 * HALO_ROWS, D_MODEL), F32),
                        pltpu.VMEM((SUBLANES, SHIFT_ROWS, CONV_COLS), F32),
                        pltpu.VMEM((ROW_TILE, D_MODEL), F32)],
        compiler_params=_params(2),
        name="conv_mix",
    )(x, u, u, u, w_dw, b_dw, ln_g, ln_b, w_pw2, b_pw2, g_post)


def _qkv_kernel(x_ref, g_ref, w_ref, o_ref):
    q_scale = HEAD_DIM ** -0.5 * LOG2_E
    for rows in _sub_blocks():
        h = _rms(x_ref[rows, :], g_ref[...], RMS_EPS).astype(BF16)
        for part in range(3):
            cols = slice(part * QK_WIDTH, (part + 1) * QK_WIDTH)
            y = jnp.dot(h, w_ref[:, cols], preferred_element_type=F32)
            if part == 0:
                y = y * q_scale
            o_ref[rows, cols] = y.astype(BF16)


def _qkv(x, g, w_qkv, layer, ia):
    m = x.shape[0]
    return pl.pallas_call(
        _qkv_kernel,
        grid=(m // FFN_ROW_TILE,),
        in_specs=[pl.BlockSpec((FFN_ROW_TILE, D_MODEL), lambda i: (i, 0)),
                  _const_spec((1, D_MODEL), (layer,)),
                  _const_spec((D_MODEL, 3 * QK_WIDTH), (ia,))],
        out_specs=pl.BlockSpec((FFN_ROW_TILE, 3 * QK_WIDTH), lambda i: (i, 0)),
        out_shape=jax.ShapeDtypeStruct((m, 3 * QK_WIDTH), BF16),
        compiler_params=_params(1),
        name="attn_qkv",
    )(x, g, w_qkv)


def _bias_lanes(lane_rel, pos, coef):
    piece = lane_rel % BIAS_SPLIT
    coef_lane = jnp.where(piece == 0, coef[0], jnp.where(piece == 1, coef[1], coef[2]))
    pos_lo = (pos % POS_RADIX).astype(F32)
    pos_hi = (pos - pos % POS_RADIX).astype(F32)
    pos_lane = jnp.where(lane_rel % (2 * BIAS_SPLIT) < BIAS_SPLIT, pos_lo, pos_hi)
    return pos_lane, coef_lane


def _attn_kernel(slope_ref, lam_ref, subg_ref, q_ref, k_ref, v_ref, o_ref,
                 vt_ref, ka_ref, diag_ref, kmax_ref, fast_ref, acc_ref, *, lambda_init, seq):
    step = pl.program_id(2)
    n_chunks = seq // K_TILE
    coef = slope_ref[:, :1] * LOG2_E
    pieces = []
    rest = coef
    for _ in range(BIAS_SPLIT):
        piece = rest.astype(BF16).astype(F32)
        pieces.append(piece)
        rest = rest - piece

    def side_lane_index(shape, mp):
        lane = lax.broadcasted_iota(jnp.int32, shape, 1)
        lane_rel = lane - (HEAD_DIM if mp == 0 else 0)
        return lane, lane_rel, (lane_rel >= 0) & (lane_rel < 4 * BIAS_SPLIT)

    @pl.when(step == 0)
    def _per_head_setup():
        ones_row = lax.broadcasted_iota(jnp.int32, (BF16_ROWS, K_TILE), 0) == 0
        for c in range(n_chunks):
            vt_ref[c, :HEAD_WIDTH] = v_ref[c * K_TILE:(c + 1) * K_TILE, :].astype(F32).T.astype(BF16)
            vt_ref[c, HEAD_WIDTH:] = ones_row.astype(BF16)
        jpos = lax.broadcasted_iota(jnp.int32, (K_TILE, HEAD_WIDTH), 0)
        for mp in range(2):
            lane, lane_rel, is_side = side_lane_index((K_TILE, HEAD_WIDTH), mp)
            pos_lane, coef_lane = _bias_lanes(lane_rel, jpos, pieces)
            side = jnp.where(lane_rel < 2 * BIAS_SPLIT, coef_lane, pos_lane)
            side = jnp.where(is_side, side, 0.0).astype(BF16)
            own = (lane < HEAD_DIM) if mp == 0 else (lane >= HEAD_DIM)
            k_sq_max = None
            for c in range(n_chunks):
                k = k_ref[c * K_TILE:(c + 1) * K_TILE, :]
                ka_ref[mp, c] = jnp.where(own, k, side)
                k_sq = jnp.sum(jnp.where(own, jnp.square(k.astype(F32)), 0.0), axis=1, keepdims=True)
                k_sq = jnp.max(k_sq, axis=0, keepdims=True)
                k_sq_max = k_sq if k_sq_max is None else jnp.maximum(k_sq_max, k_sq)
            kmax_ref[mp] = jnp.broadcast_to(jnp.sqrt(k_sq_max), kmax_ref.shape[1:])
        rel = (lax.broadcasted_iota(jnp.int32, (K_TILE, Q_TILE), 0)
               - lax.broadcasted_iota(jnp.int32, (K_TILE, Q_TILE), 1))
        diag_ref[...] = jnp.abs(rel).astype(F32) * coef

    def query_operands(q):
        ipos = lax.broadcasted_iota(jnp.int32, q.shape, 0)
        q_own, q_after, q_before = [], [], []
        for mp in range(2):
            lane, lane_rel, is_side = side_lane_index(q.shape, mp)
            pos_lane, coef_lane = _bias_lanes(lane_rel, ipos, pieces)
            side = jnp.where(lane_rel < 2 * BIAS_SPLIT, pos_lane, -coef_lane)
            side = jnp.where(is_side, side, 0.0)
            own = (lane < HEAD_DIM) if mp == 0 else (lane >= HEAD_DIM)
            q_own.append(jnp.where(own, q, jnp.zeros_like(q)))
            q_after.append(jnp.where(own, q, side.astype(BF16)))
            q_before.append(jnp.where(own, q, (-side).astype(BF16)))
        return q_own, q_after, q_before

    def fast_tile(tile, q):
        q_own, q_after, q_before = query_operands(q)

        def scores(t):
            chunk = tile + t
            wrapped = chunk >= n_chunks
            chunk = jnp.where(wrapped, chunk - n_chunks, chunk)
            dist = jnp.where(wrapped, n_chunks - t, t)
            const = coef * (dist * K_TILE).astype(F32)
            out = []
            for mp in range(2):
                if t == 0:
                    qm = q_own[mp]
                else:
                    qm = jnp.where(wrapped, q_before[mp], q_after[mp])
                s = lax.dot_general(ka_ref[mp, chunk], qm, (((1,), (1,)), ((), ())),
                                    preferred_element_type=F32)
                if t == 0:
                    s = s - diag_ref[...]
                out.append(s)
            return out, const, chunk

        s_own, _, own_chunk = scores(0)
        ref_max = [jnp.max(s, axis=0, keepdims=True) for s in s_own]
        acc_own = [jnp.dot(vt_ref[own_chunk], jnp.exp2(s - mx).astype(BF16),
                           preferred_element_type=F32) for s, mx in zip(s_own, ref_max)]

        bound = None
        for mp in range(2):
            q_sq = jnp.square(q_own[mp].astype(F32))
            q_norm = jnp.sqrt(jnp.max(jnp.sum(q_sq, axis=1, keepdims=True), axis=0, keepdims=True))
            b_mp = q_norm * kmax_ref[mp][:1, :1]
            bound = b_mp if bound is None else jnp.maximum(bound, b_mp)
        min_ref = jnp.min(jnp.minimum(ref_max[0], ref_max[1]), axis=1, keepdims=True)
        margin = bound * NORM_SLACK - min_ref + EXP2_UNDERFLOW

        def reaches(r):
            return jnp.max((coef * (K_TILE * r) >= margin).astype(F32)) > 0.0

        def visit(ts):
            acc = list(acc_own)
            if ts:
                ahead = scores(ts[0])
            for n, t in enumerate(ts):
                s_cur, const, chunk = ahead
                if n + 1 < len(ts):
                    ahead = scores(ts[n + 1])
                vt = vt_ref[chunk]
                for mp in range(2):
                    p = jnp.exp2(s_cur[mp] - (ref_max[mp] + const)).astype(BF16)
                    acc[mp] = acc[mp] + jnp.dot(vt, p, preferred_element_type=F32)
            fast_ref[0] = acc[0]
            fast_ref[1] = acc[1]

        def within(r):
            return [t for t in range(1, n_chunks) if t <= r or n_chunks - t <= r]

        taken = None
        for r in VISIT_RADII:
            ok = reaches(r)
            run = ok if taken is None else jnp.logical_and(ok, jnp.logical_not(taken))
            pl.when(run)(functools.partial(visit, within(r)))
            taken = ok if taken is None else jnp.logical_or(taken, ok)
        pl.when(jnp.logical_not(taken))(functools.partial(visit, within(n_chunks)))
        return [fast_ref[0], fast_ref[1]]

    def exact_tile(tile, q):
        q_own = query_operands(q)[0]
        kpos = lax.broadcasted_iota(jnp.int32, (K_TILE, Q_TILE), 0)
        qpos = lax.broadcasted_iota(jnp.int32, (K_TILE, Q_TILE), 1) + tile * Q_TILE

        def chunk_scores(c):
            start = pl.multiple_of(c * K_TILE, K_TILE)
            k = k_ref[pl.ds(start, K_TILE), :]
            bias = jnp.abs(kpos + start - qpos).astype(F32) * coef
            return [lax.dot_general(k, qm, (((1,), (1,)), ((), ())),
                                    preferred_element_type=F32) - bias for qm in q_own]

        def max_pass(c, m):
            return tuple(jnp.maximum(mm, jnp.max(s, axis=0, keepdims=True))
                         for mm, s in zip(m, chunk_scores(c)))

        neg_inf = jnp.full((1, Q_TILE), -jnp.inf, F32)
        true_max = lax.fori_loop(0, n_chunks, max_pass, (neg_inf, neg_inf))
        acc_ref[...] = jnp.zeros(acc_ref.shape, F32)

        def exp_pass(c, carry):
            for mp, s in enumerate(chunk_scores(c)):
                p = jnp.exp2(s - true_max[mp]).astype(BF16)
                acc_ref[mp] += jnp.dot(vt_ref[c], p, preferred_element_type=F32)
            return carry

        lax.fori_loop(0, n_chunks, exp_pass, 0)
        return [acc_ref[0], acc_ref[1]]

    def finish(acc, rows):
        lq = lam_ref[...]
        lam = (jnp.exp(jnp.sum(lq[0:1] * lq[1:2], axis=-1, keepdims=True))
               - jnp.exp(jnp.sum(lq[2:3] * lq[3:4], axis=-1, keepdims=True)) + lambda_init)
        ot = [a[:HEAD_WIDTH] / a[HEAD_WIDTH:HEAD_WIDTH + 1] for a in acc]
        o = (ot[0] - lam * ot[1]).T
        o = _rms(o, subg_ref[...], SUBLN_EPS) * (1.0 - lambda_init)
        o_ref[rows, :] = o.astype(BF16)

    overflow = []
    for j in range(TILES_PER_STEP):
        rows = slice(j * Q_TILE, (j + 1) * Q_TILE)
        acc = fast_tile(step * TILES_PER_STEP + j, q_ref[rows, :])
        finish(acc, rows)
        bad = jnp.logical_not((jnp.abs(acc[0]) < F32_HUGE) & (jnp.abs(acc[1]) < F32_HUGE))
        overflow.append(jnp.max(bad.astype(F32)) > 0.0)

    for j in range(TILES_PER_STEP):
        rows = slice(j * Q_TILE, (j + 1) * Q_TILE)

        @pl.when(overflow[j])
        def _redo_exactly(j=j, rows=rows):
            finish(exact_tile(step * TILES_PER_STEP + j, q_ref[rows, :]), rows)


def _attn(qkv, slopes, lam_vecs, subln_g, lambda_init, ia):
    b, s, _ = qkv.shape
    kernel = functools.partial(_attn_kernel, lambda_init=lambda_init, seq=s)
    step_rows = TILES_PER_STEP * Q_TILE
    return pl.pallas_call(
        kernel,
        grid=(b, N_HEADS, s // step_rows),
        in_specs=[pl.BlockSpec((None, 1, HEAD_WIDTH), lambda bi, h, i: (h, 0, 0)),
                  _const_spec((4, HEAD_DIM), (ia,)), _const_spec((1, HEAD_WIDTH), (ia,)),
                  pl.BlockSpec((None, step_rows, HEAD_WIDTH), lambda bi, h, i: (bi, i, h)),
                  pl.BlockSpec((None, s, HEAD_WIDTH), lambda bi, h, i: (bi, 0, N_HEADS + h)),
                  pl.BlockSpec((None, s, HEAD_WIDTH), lambda bi, h, i: (bi, 0, 2 * N_HEADS + h))],
        out_specs=pl.BlockSpec((None, step_rows, HEAD_WIDTH), lambda bi, h, i: (bi, i, h)),
        out_shape=jax.ShapeDtypeStruct((b, s, QK_WIDTH), BF16),
        scratch_shapes=[pltpu.VMEM((s // K_TILE, HEAD_WIDTH + BF16_ROWS, K_TILE), BF16),
                        pltpu.VMEM((2, s // K_TILE, K_TILE, HEAD_WIDTH), BF16),
                        pltpu.VMEM((K_TILE, Q_TILE), F32),
                        pltpu.VMEM((2, SUBLANES, HEAD_WIDTH), F32),
                        pltpu.VMEM((2, HEAD_WIDTH + BF16_ROWS, Q_TILE), F32),
                        pltpu.VMEM((2, HEAD_WIDTH + BF16_ROWS, Q_TILE), F32)],
        compiler_params=_params(3),
        name="attn_core",
    )(slopes, lam_vecs, subln_g, qkv, qkv, qkv)


def _oproj_kernel(x_ref, o_ref, w_ref, g_ref, y_ref):
    for rows in _sub_blocks():
        mix = jnp.dot(o_ref[rows, :], w_ref[...], preferred_element_type=F32)
        y_ref[rows, :] = x_ref[rows, :] + _rms(mix, g_ref[...], RMS_EPS)


def _oproj(x, o, w_o, g_post, layer, ia):
    m = x.shape[0]
    row = pl.BlockSpec((FFN_ROW_TILE, D_MODEL), lambda i: (i, 0))
    return pl.pallas_call(
        _oproj_kernel,
        grid=(m // FFN_ROW_TILE,),
        in_specs=[row, row, _const_spec((QK_WIDTH, D_MODEL), (ia,)),
                  _const_spec((1, D_MODEL), (layer,))],
        out_specs=row,
        out_shape=jax.ShapeDtypeStruct((m, D_MODEL), F32),
        compiler_params=_params(1),
        name="attn_oproj",
    )(x, o, w_o, g_post)


def _lambda_init(layer_idx):
    return 0.8 - 0.6 * math.exp(-0.3 * layer_idx)


def kernel(x, ffn_norm_pre, ffn_norm_post, ffn_w_gate, ffn_w_up, ffn_w_down, mix_norm_pre, mix_norm_post, conv_w_pw1, conv_b_pw1, conv_w_dw, conv_b_dw, conv_ln_g, conv_ln_b, conv_w_pw2, conv_b_pw2, attn_w_qkv, attn_w_o, attn_lambda_q1, attn_lambda_k1, attn_lambda_q2, attn_lambda_k2, attn_subln_g):
    b, s, d = x.shape
    depth = ffn_norm_pre.shape[0]
    m = b * s
    assert d == D_MODEL and m % ROW_TILE == 0 and s % ROW_TILE == 0 and m % FFN_ROW_TILE == 0
    assert s % (TILES_PER_STEP * Q_TILE) == 0 and s % K_TILE == 0

    w_gate, w_up, w_down = (w.astype(BF16) for w in (ffn_w_gate, ffn_w_up, ffn_w_down))
    w_pw1, w_pw2 = conv_w_pw1.astype(BF16), conv_w_pw2.astype(BF16)
    w_qkv, w_o = attn_w_qkv.astype(BF16), attn_w_o.astype(BF16)
    g_ffn_pre, g_ffn_post = ffn_norm_pre[:, :, None, :], ffn_norm_post[:, :, None, :]
    g_mix_pre, g_mix_post = mix_norm_pre[:, None, :], mix_norm_post[:, None, :]
    b_pw1, b_dw, b_pw2 = conv_b_pw1[:, None, :], conv_b_dw[:, None, :], conv_b_pw2[:, None, :]
    ln_g, ln_b = conv_ln_g[:, None, :], conv_ln_b[:, None, :]
    subln_g = attn_subln_g[:, None, :]
    lam_vecs = jnp.stack([attn_lambda_q1, attn_lambda_k1, attn_lambda_q2, attn_lambda_k2], axis=1)
    slopes = 2.0 ** (-8.0 * (jnp.arange(N_HEADS, dtype=F32) + 1.0) / N_HEADS)
    slopes = jnp.broadcast_to(slopes[:, None, None], (N_HEADS, 1, HEAD_WIDTH))

    x = x.reshape(m, d)
    ic = ia = 0
    for i in range(depth):
        x = _ffn(x, g_ffn_pre, g_ffn_post, w_gate, w_up, w_down, (i, 0))
        if i % 2 == 0:
            u = _glu(x, g_mix_pre, w_pw1, b_pw1, i, ic)
            x = _conv(x.reshape(b, s, d), u.reshape(b, s, d), conv_w_dw, b_dw, ln_g, ln_b,
                      w_pw2, b_pw2, g_mix_post, i, ic).reshape(m, d)
            ic += 1
        else:
            qkv = _qkv(x, g_mix_pre, w_qkv, i, ia)
            o = _attn(qkv.reshape(b, s, 3 * QK_WIDTH), slopes, lam_vecs, subln_g,
                      _lambda_init(i), ia)
            x = _oproj(x, o.reshape(m, QK_WIDTH), w_o, g_mix_post, i, ia)
            ia += 1
        x = _ffn(x, g_ffn_pre, g_ffn_post, w_gate, w_up, w_down, (i, 1))
    return x.reshape(b, s, d)
```

```python
import functools
import math

import jax
import jax.numpy as jnp
from jax import lax
from jax.experimental import pallas as pl
from jax.experimental.pallas import tpu as pltpu

D_MODEL = 1024
D_FF = 2816
CONV_WIDTH = 31
CONV_PAD = (CONV_WIDTH - 1) // 2
N_HEADS = 8
HEAD_DIM = 64
HEAD_WIDTH = 2 * HEAD_DIM
QK_WIDTH = N_HEADS * HEAD_WIDTH
RMS_EPS = 1e-6
SUBLN_EPS = 1e-5
LN_EPS = 1e-5
LOG2_E = math.log2(math.e)
F32_HUGE = 3.0e38

V7X_VMEM_BYTES = 64 * 1024 * 1024
VMEM_LIMIT_BYTES = V7X_VMEM_BYTES * 7 // 8

ROW_TILE = 512
FFN_ROW_TILE = 1024
FFN_SUB_ROWS = 512
FF_CHUNK = 256
HALO_ROWS = 16
CONV_ROWS = 32
CONV_COLS = 256
SUBLANES = 8
BF16_ROWS = 16
SHIFT_ROWS = ROW_TILE + 2 * HALO_ROWS - SUBLANES
Q_TILE = 512
K_TILE = 512
TILES_PER_STEP = 2
BIAS_SPLIT = 3
POS_RADIX = 256
assert Q_TILE == K_TILE and K_TILE <= POS_RADIX * POS_RADIX

F32 = jnp.float32
BF16 = jnp.bfloat16


def _rms(x, g, eps):
    return x * lax.rsqrt(jnp.mean(x * x, axis=-1, keepdims=True) + eps) * g


def _const_spec(shape, lead=()):
    zeros = (0,) * len(shape)
    return pl.BlockSpec((None,) * len(lead) + tuple(shape), lambda *_: tuple(lead) + zeros,
                        pipeline_mode=pl.Buffered(1))


def _params(n_axes):
    return pltpu.CompilerParams(dimension_semantics=("arbitrary",) * n_axes,
                                vmem_limit_bytes=VMEM_LIMIT_BYTES)


def _sub_blocks():
    return [slice(r * FFN_SUB_ROWS, (r + 1) * FFN_SUB_ROWS)
            for r in range(FFN_ROW_TILE // FFN_SUB_ROWS)]


def _ffn_kernel(x_ref, gpre_ref, gpost_ref, wg_ref, wu_ref, wd_ref, o_ref, p_ref):
    for r, rows in enumerate(_sub_blocks()):
        x = x_ref[rows, :]
        h = _rms(x, gpre_ref[...], RMS_EPS).astype(BF16)
        for c in range(D_FF // FF_CHUNK):
            cols = slice(c * FF_CHUNK, (c + 1) * FF_CHUNK)
            g = jnp.dot(h, wg_ref[:, cols], preferred_element_type=F32)
            u = jnp.dot(h, wu_ref[:, cols], preferred_element_type=F32)
            p_ref[r, :, cols] = (g * jax.nn.sigmoid(g) * u).astype(BF16)
        f = jnp.dot(p_ref[r], wd_ref[...], preferred_element_type=F32)
        o_ref[rows, :] = x + 0.5 * _rms(f, gpost_ref[...], RMS_EPS)


def _ffn(x, g_pre, g_post, w_gate, w_up, w_down, idx):
    m = x.shape[0]
    row = pl.BlockSpec((FFN_ROW_TILE, D_MODEL), lambda i: (i, 0))
    return pl.pallas_call(
        _ffn_kernel,
        grid=(m // FFN_ROW_TILE,),
        in_specs=[row, _const_spec((1, D_MODEL), idx), _const_spec((1, D_MODEL), idx),
                  _const_spec((D_MODEL, D_FF), idx), _const_spec((D_MODEL, D_FF), idx),
                  _const_spec((D_FF, D_MODEL), idx)],
        out_specs=row,
        out_shape=jax.ShapeDtypeStruct((m, D_MODEL), F32),
        scratch_shapes=[pltpu.VMEM((FFN_ROW_TILE // FFN_SUB_ROWS, FFN_SUB_ROWS, D_FF), BF16)],
        compiler_params=_params(1),
        name="ffn",
    )(x, g_pre, g_post, w_gate, w_up, w_down)


def _glu_kernel(x_ref, g_ref, w_ref, b_ref, u_ref):
    for rows in _sub_blocks():
        h = _rms(x_ref[rows, :], g_ref[...], RMS_EPS).astype(BF16)
        a = jnp.dot(h, w_ref[:, :D_MODEL], preferred_element_type=F32) + b_ref[:, :D_MODEL]
        t = jnp.dot(h, w_ref[:, D_MODEL:], preferred_element_type=F32) + b_ref[:, D_MODEL:]
        u_ref[rows, :] = a * jax.nn.sigmoid(t)


def _glu(x, g, w_pw1, b_pw1, layer, ic):
    m = x.shape[0]
    row = pl.BlockSpec((FFN_ROW_TILE, D_MODEL), lambda i: (i, 0))
    return pl.pallas_call(
        _glu_kernel,
        grid=(m // FFN_ROW_TILE,),
        in_specs=[row, _const_spec((1, D_MODEL), (layer,)),
                  _const_spec((D_MODEL, 2 * D_MODEL), (ic,)),
                  _const_spec((1, 2 * D_MODEL), (ic,))],
        out_specs=row,
        out_shape=jax.ShapeDtypeStruct((m, D_MODEL), F32),
        compiler_params=_params(1),
        name="conv_glu",
    )(x, g, w_pw1, b_pw1)


def _conv_kernel(x_ref, u_ref, prev_ref, next_ref, wdw_ref, bdw_ref, lng_ref, lnb_ref,
                 w2_ref, b2_ref, gpost_ref, o_ref, ubuf_ref, sh_ref, c_ref, wrep_ref):
    i = pl.program_id(1)
    n = pl.num_programs(1)
    ubuf_ref[:HALO_ROWS, :] = jnp.where(i > 0, prev_ref[...], 0.0)
    ubuf_ref[HALO_ROWS:HALO_ROWS + ROW_TILE, :] = u_ref[...]
    ubuf_ref[HALO_ROWS + ROW_TILE:, :] = jnp.where(i < n - 1, next_ref[...], 0.0)

    @pl.when((pl.program_id(0) == 0) & (i == 0))
    def _replicate_taps():
        for k in range(CONV_WIDTH):
            wrep_ref[k] = jnp.broadcast_to(wdw_ref[k:k + 1, :], (SUBLANES, D_MODEL))

    groups = CONV_ROWS // SUBLANES
    for cb in range(D_MODEL // CONV_COLS):
        cols = slice(cb * CONV_COLS, (cb + 1) * CONV_COLS)
        for phase in range(1, SUBLANES):
            sh_ref[phase] = ubuf_ref[phase:phase + SHIFT_ROWS, cols]

        def row_block(r, carry, cols=cols):
            base = pl.multiple_of(r * CONV_ROWS, CONV_ROWS)
            acc = [jnp.zeros((SUBLANES, CONV_COLS), F32) for _ in range(groups)]
            for k in range(CONV_WIDTH):
                step, phase = divmod(HALO_ROWS - CONV_PAD + k, SUBLANES)
                w = wrep_ref[k, :, cols]
                for g in range(groups):
                    rows = pl.ds(base + (step + g) * SUBLANES, SUBLANES)
                    win = ubuf_ref[rows, cols] if phase == 0 else sh_ref[phase, rows, :]
                    acc[g] = acc[g] + win * w
            for g in range(groups):
                c_ref[pl.ds(base + g * SUBLANES, SUBLANES), cols] = acc[g]
            return carry

        lax.fori_loop(0, ROW_TILE // CONV_ROWS, row_block, 0)

    c = c_ref[...] + bdw_ref[...]
    mu = jnp.mean(c, axis=-1, keepdims=True)
    d = c - mu
    var = jnp.mean(d * d, axis=-1, keepdims=True)
    y = d * lax.rsqrt(var + LN_EPS) * lng_ref[...] + lnb_ref[...]
    y = (y * jax.nn.sigmoid(y)).astype(BF16)
    mix = jnp.dot(y, w2_ref[...], preferred_element_type=F32) + b2_ref[...]
    o_ref[...] = x_ref[...] + _rms(mix, gpost_ref[...], RMS_EPS)


def _conv(x, u, w_dw, b_dw, ln_g, ln_b, w_pw2, b_pw2, g_post, layer, ic):
    b, s, _ = x.shape
    tiles = s // ROW_TILE
    halo_per_tile = ROW_TILE // HALO_ROWS
    last_halo = s // HALO_ROWS - 1
    row = pl.BlockSpec((None, ROW_TILE, D_MODEL), lambda bi, i: (bi, i, 0))
    prev = pl.BlockSpec((None, HALO_ROWS, D_MODEL),
                        lambda bi, i: (bi, jnp.maximum(i * halo_per_tile - 1, 0), 0))
    nxt = pl.BlockSpec((None, HALO_ROWS, D_MODEL),
                       lambda bi, i: (bi, jnp.minimum((i + 1) * halo_per_tile, last_halo), 0))
    vec = _const_spec((1, D_MODEL), (ic,))
    return pl.pallas_call(
        _conv_kernel,
        grid=(b, tiles),
        in_specs=[row, row, prev, nxt, _const_spec((CONV_WIDTH, D_MODEL), (ic,)), vec, vec, vec,
                  _const_spec((D_MODEL, D_MODEL), (ic,)), vec,
                  _const_spec((1, D_MODEL), (layer,))],
        out_specs=row,
        out_shape=jax.ShapeDtypeStruct((b, s, D_MODEL), F32),
        scratch_shapes=[pltpu.VMEM((ROW_TILE + 2 * HALO_ROWS, D_MODEL), F32),
                        pltpu.VMEM((SUBLANES, SHIFT_ROWS, CONV_COLS), F32),
                        pltpu.VMEM((ROW_TILE, D_MODEL), F32),
                        pltpu.VMEM((CONV_WIDTH, SUBLANES, D_MODEL), F32)],
        compiler_params=_params(2),
        name="conv_mix",
    )(x, u, u, u, w_dw, b_dw, ln_g, ln_b, w_pw2, b_pw2, g_post)


def _qkv_kernel(x_ref, g_ref, w_ref, o_ref):
    q_scale = HEAD_DIM ** -0.5 * LOG2_E
    for rows in _sub_blocks():
        h = _rms(x_ref[rows, :], g_ref[...], RMS_EPS).astype(BF16)
        for part in range(3):
            cols = slice(part * QK_WIDTH, (part + 1) * QK_WIDTH)
            y = jnp.dot(h, w_ref[:, cols], preferred_element_type=F32)
            if part == 0:
                y = y * q_scale
            o_ref[rows, cols] = y.astype(BF16)


def _qkv(x, g, w_qkv, layer, ia):
    m = x.shape[0]
    return pl.pallas_call(
        _qkv_kernel,
        grid=(m // FFN_ROW_TILE,),
        in_specs=[pl.BlockSpec((FFN_ROW_TILE, D_MODEL), lambda i: (i, 0)),
                  _const_spec((1, D_MODEL), (layer,)),
                  _const_spec((D_MODEL, 3 * QK_WIDTH), (ia,))],
        out_specs=pl.BlockSpec((FFN_ROW_TILE, 3 * QK_WIDTH), lambda i: (i, 0)),
        out_shape=jax.ShapeDtypeStruct((m, 3 * QK_WIDTH), BF16),
        compiler_params=_params(1),
        name="attn_qkv",
    )(x, g, w_qkv)


def _bias_lanes(lane_rel, pos, coef):
    piece = lane_rel % BIAS_SPLIT
    coef_lane = jnp.where(piece == 0, coef[0], jnp.where(piece == 1, coef[1], coef[2]))
    pos_lo = (pos % POS_RADIX).astype(F32)
    pos_hi = (pos - pos % POS_RADIX).astype(F32)
    pos_lane = jnp.where(lane_rel % (2 * BIAS_SPLIT) < BIAS_SPLIT, pos_lo, pos_hi)
    return pos_lane, coef_lane


def _attn_kernel(slope_ref, lam_ref, subg_ref, q_ref, k_ref, v_ref, o_ref,
                 vt_ref, ka_ref, diag_ref, acc_ref, *, lambda_init, seq):
    step = pl.program_id(2)
    n_chunks = seq // K_TILE
    coef = slope_ref[:, :1] * LOG2_E
    pieces = []
    rest = coef
    for _ in range(BIAS_SPLIT):
        piece = rest.astype(BF16).astype(F32)
        pieces.append(piece)
        rest = rest - piece

    def side_lane_index(shape, mp):
        lane = lax.broadcasted_iota(jnp.int32, shape, 1)
        lane_rel = lane - (HEAD_DIM if mp == 0 else 0)
        return lane, lane_rel, (lane_rel >= 0) & (lane_rel < 4 * BIAS_SPLIT)

    @pl.when(step == 0)
    def _per_head_setup():
        ones_row = lax.broadcasted_iota(jnp.int32, (BF16_ROWS, K_TILE), 0) == 0
        for c in range(n_chunks):
            vt_ref[c, :HEAD_WIDTH] = v_ref[c * K_TILE:(c + 1) * K_TILE, :].astype(F32).T.astype(BF16)
            vt_ref[c, HEAD_WIDTH:] = ones_row.astype(BF16)
        jpos = lax.broadcasted_iota(jnp.int32, (K_TILE, HEAD_WIDTH), 0)
        for mp in range(2):
            lane, lane_rel, is_side = side_lane_index((K_TILE, HEAD_WIDTH), mp)
            pos_lane, coef_lane = _bias_lanes(lane_rel, jpos, pieces)
            side = jnp.where(lane_rel < 2 * BIAS_SPLIT, coef_lane, pos_lane)
            side = jnp.where(is_side, side, 0.0).astype(BF16)
            own = (lane < HEAD_DIM) if mp == 0 else (lane >= HEAD_DIM)
            for c in range(n_chunks):
                k = k_ref[c * K_TILE:(c + 1) * K_TILE, :]
                ka_ref[mp, c] = jnp.where(own, k, side)
        rel = (lax.broadcasted_iota(jnp.int32, (K_TILE, Q_TILE), 0)
               - lax.broadcasted_iota(jnp.int32, (K_TILE, Q_TILE), 1))
        diag_ref[...] = jnp.abs(rel).astype(F32) * coef

    def query_operands(q):
        ipos = lax.broadcasted_iota(jnp.int32, q.shape, 0)
        q_own, q_after, q_before = [], [], []
        for mp in range(2):
            lane, lane_rel, is_side = side_lane_index(q.shape, mp)
            pos_lane, coef_lane = _bias_lanes(lane_rel, ipos, pieces)
            side = jnp.where(lane_rel < 2 * BIAS_SPLIT, pos_lane, -coef_lane)
            side = jnp.where(is_side, side, 0.0)
            own = (lane < HEAD_DIM) if mp == 0 else (lane >= HEAD_DIM)
            q_own.append(jnp.where(own, q, jnp.zeros_like(q)))
            q_after.append(jnp.where(own, q, side.astype(BF16)))
            q_before.append(jnp.where(own, q, (-side).astype(BF16)))
        return q_own, q_after, q_before

    def fast_tile(tile, q):
        q_own, q_after, q_before = query_operands(q)

        def scores(t):
            chunk = tile + t
            wrapped = chunk >= n_chunks
            chunk = jnp.where(wrapped, chunk - n_chunks, chunk)
            dist = jnp.where(wrapped, n_chunks - t, t)
            const = coef * (dist * K_TILE).astype(F32)
            out = []
            for mp in range(2):
                if t == 0:
                    qm = q_own[mp]
                else:
                    qm = jnp.where(wrapped, q_before[mp], q_after[mp])
                s = lax.dot_general(ka_ref[mp, chunk], qm, (((1,), (1,)), ((), ())),
                                    preferred_element_type=F32)
                if t == 0:
                    s = s - diag_ref[...]
                out.append(s)
            return out, const, chunk

        ahead = scores(0)
        ref_max = [jnp.max(s, axis=0, keepdims=True) for s in ahead[0]]
        acc = [None, None]
        for t in range(n_chunks):
            s_cur, const, chunk = ahead
            if t + 1 < n_chunks:
                ahead = scores(t + 1)
            vt = vt_ref[chunk]
            for mp in range(2):
                p = jnp.exp2(s_cur[mp] - (ref_max[mp] + const)).astype(BF16)
                pv = jnp.dot(vt, p, preferred_element_type=F32)
                acc[mp] = pv if t == 0 else acc[mp] + pv
        return acc

    def exact_tile(tile, q):
        q_own = query_operands(q)[0]
        kpos = lax.broadcasted_iota(jnp.int32, (K_TILE, Q_TILE), 0)
        qpos = lax.broadcasted_iota(jnp.int32, (K_TILE, Q_TILE), 1) + tile * Q_TILE

        def chunk_scores(c):
            start = pl.multiple_of(c * K_TILE, K_TILE)
            k = k_ref[pl.ds(start, K_TILE), :]
            bias = jnp.abs(kpos + start - qpos).astype(F32) * coef
            return [lax.dot_general(k, qm, (((1,), (1,)), ((), ())),
                                    preferred_element_type=F32) - bias for qm in q_own]

        def max_pass(c, m):
            return tuple(jnp.maximum(mm, jnp.max(s, axis=0, keepdims=True))
                         for mm, s in zip(m, chunk_scores(c)))

        neg_inf = jnp.full((1, Q_TILE), -jnp.inf, F32)
        true_max = lax.fori_loop(0, n_chunks, max_pass, (neg_inf, neg_inf))
        acc_ref[...] = jnp.zeros(acc_ref.shape, F32)

        def exp_pass(c, carry):
            for mp, s in enumerate(chunk_scores(c)):
                p = jnp.exp2(s - true_max[mp]).astype(BF16)
                acc_ref[mp] += jnp.dot(vt_ref[c], p, preferred_element_type=F32)
            return carry

        lax.fori_loop(0, n_chunks, exp_pass, 0)
        return [acc_ref[0], acc_ref[1]]

    def finish(acc, rows):
        lq = lam_ref[...]
        lam = (jnp.exp(jnp.sum(lq[0:1] * lq[1:2], axis=-1, keepdims=True))
               - jnp.exp(jnp.sum(lq[2:3] * lq[3:4], axis=-1, keepdims=True)) + lambda_init)
        ot = [a[:HEAD_WIDTH] / a[HEAD_WIDTH:HEAD_WIDTH + 1] for a in acc]
        o = (ot[0] - lam * ot[1]).T
        o = _rms(o, subg_ref[...], SUBLN_EPS) * (1.0 - lambda_init)
        o_ref[rows, :] = o.astype(BF16)

    overflow = []
    for j in range(TILES_PER_STEP):
        rows = slice(j * Q_TILE, (j + 1) * Q_TILE)
        acc = fast_tile(step * TILES_PER_STEP + j, q_ref[rows, :])
        finish(acc, rows)
        bad = jnp.logical_not((jnp.abs(acc[0]) < F32_HUGE) & (jnp.abs(acc[1]) < F32_HUGE))
        overflow.append(jnp.max(bad.astype(F32)) > 0.0)

    for j in range(TILES_PER_STEP):
        rows = slice(j * Q_TILE, (j + 1) * Q_TILE)

        @pl.when(overflow[j])
        def _redo_exactly(j=j, rows=rows):
            finish(exact_tile(step * TILES_PER_STEP + j, q_ref[rows, :]), rows)


def _attn(qkv, slopes, lam_vecs, subln_g, lambda_init, ia):
    b, s, _ = qkv.shape
    kernel = functools.partial(_attn_kernel, lambda_init=lambda_init, seq=s)
    step_rows = TILES_PER_STEP * Q_TILE
    return pl.pallas_call(
        kernel,
        grid=(b, N_HEADS, s // step_rows),
        in_specs=[pl.BlockSpec((None, 1, HEAD_WIDTH), lambda bi, h, i: (h, 0, 0)),
                  _const_spec((4, HEAD_DIM), (ia,)), _const_spec((1, HEAD_WIDTH), (ia,)),
                  pl.BlockSpec((None, step_rows, HEAD_WIDTH), lambda bi, h, i: (bi, i, h)),
                  pl.BlockSpec((None, s, HEAD_WIDTH), lambda bi, h, i: (bi, 0, N_HEADS + h)),
                  pl.BlockSpec((None, s, HEAD_WIDTH), lambda bi, h, i: (bi, 0, 2 * N_HEADS + h))],
        out_specs=pl.BlockSpec((None, step_rows, HEAD_WIDTH), lambda bi, h, i: (bi, i, h)),
        out_shape=jax.ShapeDtypeStruct((b, s, QK_WIDTH), BF16),
        scratch_shapes=[pltpu.VMEM((s // K_TILE, HEAD_WIDTH + BF16_ROWS, K_TILE), BF16),
                        pltpu.VMEM((2, s // K_TILE, K_TILE, HEAD_WIDTH), BF16),
                        pltpu.VMEM((K_TILE, Q_TILE), F32),
                        pltpu.VMEM((2, HEAD_WIDTH + BF16_ROWS, Q_TILE), F32)],
        compiler_params=_params(3),
        name="attn_core",
    )(slopes, lam_vecs, subln_g, qkv, qkv, qkv)


def _oproj_kernel(x_ref, o_ref, w_ref, g_ref, y_ref):
    for rows in _sub_blocks():
        mix = jnp.dot(o_ref[rows, :], w_ref[...], preferred_element_type=F32)
        y_ref[rows, :] = x_ref[rows, :] + _rms(mix, g_ref[...], RMS_EPS)


def _oproj(x, o, w_o, g_post, layer, ia):
    m = x.shape[0]
    row = pl.BlockSpec((FFN_ROW_TILE, D_MODEL), lambda i: (i, 0))
    return pl.pallas_call(
        _oproj_kernel,
        grid=(m // FFN_ROW_TILE,),
        in_specs=[row, row, _const_spec((QK_WIDTH, D_MODEL), (ia,)),
                  _const_spec((1, D_MODEL), (layer,))],
        out_specs=row,
        out_shape=jax.ShapeDtypeStruct((m, D_MODEL), F32),
        compiler_params=_params(1),
        name="attn_oproj",
    )(x, o, w_o, g_post)


def _lambda_init(layer_idx):
    return 0.8 - 0.6 * math.exp(-0.3 * layer_idx)


def kernel(x, ffn_norm_pre, ffn_norm_post, ffn_w_gate, ffn_w_up, ffn_w_down, mix_norm_pre, mix_norm_post, conv_w_pw1, conv_b_pw1, conv_w_dw, conv_b_dw, conv_ln_g, conv_ln_b, conv_w_pw2, conv_b_pw2, attn_w_qkv, attn_w_o, attn_lambda_q1, attn_lambda_k1, attn_lambda_q2, attn_lambda_k2, attn_subln_g):
    b, s, d = x.shape
    depth = ffn_norm_pre.shape[0]
    m = b * s
    assert d == D_MODEL and m % ROW_TILE == 0 and s % ROW_TILE == 0 and m % FFN_ROW_TILE == 0
    assert s % (TILES_PER_STEP * Q_TILE) == 0 and s % K_TILE == 0

    w_gate, w_up, w_down = (w.astype(BF16) for w in (ffn_w_gate, ffn_w_up, ffn_w_down))
    w_pw1, w_pw2 = conv_w_pw1.astype(BF16), conv_w_pw2.astype(BF16)
    w_qkv, w_o = attn_w_qkv.astype(BF16), attn_w_o.astype(BF16)
    g_ffn_pre, g_ffn_post = ffn_norm_pre[:, :, None, :], ffn_norm_post[:, :, None, :]
    g_mix_pre, g_mix_post = mix_norm_pre[:, None, :], mix_norm_post[:, None, :]
    b_pw1, b_dw, b_pw2 = conv_b_pw1[:, None, :], conv_b_dw[:, None, :], conv_b_pw2[:, None, :]
    ln_g, ln_b = conv_ln_g[:, None, :], conv_ln_b[:, None, :]
    subln_g = attn_subln_g[:, None, :]
    lam_vecs = jnp.stack([attn_lambda_q1, attn_lambda_k1, attn_lambda_q2, attn_lambda_k2], axis=1)
    slopes = 2.0 ** (-8.0 * (jnp.arange(N_HEADS, dtype=F32) + 1.0) / N_HEADS)
    slopes = jnp.broadcast_to(slopes[:, None, None], (N_HEADS, 1, HEAD_WIDTH))

    x = x.reshape(m, d)
    ic = ia = 0
    for i in range(depth):
        x = _ffn(x, g_ffn_pre, g_ffn_post, w_gate, w_up, w_down, (i, 0))
        if i % 2 == 0:
            u = _glu(x, g_mix_pre, w_pw1, b_pw1, i, ic)
            x = _conv(x.reshape(b, s, d), u.reshape(b, s, d), conv_w_dw, b_dw, ln_g, ln_b,
                      w_pw2, b_pw2, g_mix_post, i, ic).reshape(m, d)
            ic += 1
        else:
            qkv = _qkv(x, g_mix_pre, w_qkv, i, ia)
            o = _attn(qkv.reshape(b, s, 3 * QK_WIDTH), slopes, lam_vecs, subln_g,
                      _lambda_init(i), ia)
            x = _oproj(x, o.reshape(m, QK_WIDTH), w_o, g_mix_post, i, ia)
            ia += 1
        x = _ffn(x, g_ffn_pre, g_ffn_post, w_gate, w_up, w_down, (i, 1))
    return x.reshape(b, s, d)
```

```python
import functools
import math

import jax
import jax.numpy as jnp
from jax import lax
from jax.experimental import pallas as pl
from jax.experimental.pallas import tpu as pltpu

D_MODEL = 1024
D_FF = 2816
CONV_WIDTH = 31
CONV_PAD = (CONV_WIDTH - 1) // 2
N_HEADS = 8
HEAD_DIM = 64
HEAD_WIDTH = 2 * HEAD_DIM
QK_WIDTH = N_HEADS * HEAD_WIDTH
RMS_EPS = 1e-6
SUBLN_EPS = 1e-5
LN_EPS = 1e-5
LOG2_E = math.log2(math.e)
F32_HUGE = 3.0e38

V7X_VMEM_BYTES = 64 * 1024 * 1024
VMEM_LIMIT_BYTES = V7X_VMEM_BYTES * 7 // 8

ROW_TILE = 512
FFN_ROW_TILE = 1024
FFN_SUB_ROWS = 256
FF_CHUNK = 256
HALO_ROWS = 16
CONV_ROWS = 32
CONV_COLS = 256
SUBLANES = 8
BF16_ROWS = 16
SHIFT_ROWS = ROW_TILE + 2 * HALO_ROWS - SUBLANES
Q_TILE = 512
K_TILE = 512
TILES_PER_STEP = 2
BIAS_SPLIT = 3
POS_RADIX = 256
assert Q_TILE == K_TILE and K_TILE <= POS_RADIX * POS_RADIX

F32 = jnp.float32
BF16 = jnp.bfloat16


def _rms(x, g, eps):
    return x * lax.rsqrt(jnp.mean(x * x, axis=-1, keepdims=True) + eps) * g


def _const_spec(shape, lead=()):
    zeros = (0,) * len(shape)
    return pl.BlockSpec((None,) * len(lead) + tuple(shape), lambda *_: tuple(lead) + zeros,
                        pipeline_mode=pl.Buffered(1))


def _params(n_axes):
    return pltpu.CompilerParams(dimension_semantics=("arbitrary",) * n_axes,
                                vmem_limit_bytes=VMEM_LIMIT_BYTES)


def _sub_blocks():
    return [slice(r * FFN_SUB_ROWS, (r + 1) * FFN_SUB_ROWS)
            for r in range(FFN_ROW_TILE // FFN_SUB_ROWS)]


def _ffn_kernel(x_ref, gpre_ref, gpost_ref, wg_ref, wu_ref, wd_ref, o_ref, p_ref):
    for r, rows in enumerate(_sub_blocks()):
        x = x_ref[rows, :]
        h = _rms(x, gpre_ref[...], RMS_EPS).astype(BF16)
        for c in range(D_FF // FF_CHUNK):
            cols = slice(c * FF_CHUNK, (c + 1) * FF_CHUNK)
            g = jnp.dot(h, wg_ref[:, cols], preferred_element_type=F32)
            u = jnp.dot(h, wu_ref[:, cols], preferred_element_type=F32)
            p_ref[r, :, cols] = (g * jax.nn.sigmoid(g) * u).astype(BF16)
        f = jnp.dot(p_ref[r], wd_ref[...], preferred_element_type=F32)
        o_ref[rows, :] = x + 0.5 * _rms(f, gpost_ref[...], RMS_EPS)


def _ffn(x, g_pre, g_post, w_gate, w_up, w_down, idx):
    m = x.shape[0]
    row = pl.BlockSpec((FFN_ROW_TILE, D_MODEL), lambda i: (i, 0))
    return pl.pallas_call(
        _ffn_kernel,
        grid=(m // FFN_ROW_TILE,),
        in_specs=[row, _const_spec((1, D_MODEL), idx), _const_spec((1, D_MODEL), idx),
                  _const_spec((D_MODEL, D_FF), idx), _const_spec((D_MODEL, D_FF), idx),
                  _const_spec((D_FF, D_MODEL), idx)],
        out_specs=row,
        out_shape=jax.ShapeDtypeStruct((m, D_MODEL), F32),
        scratch_shapes=[pltpu.VMEM((FFN_ROW_TILE // FFN_SUB_ROWS, FFN_SUB_ROWS, D_FF), BF16)],
        compiler_params=_params(1),
        name="ffn",
    )(x, g_pre, g_post, w_gate, w_up, w_down)


def _glu_kernel(x_ref, g_ref, w_ref, b_ref, u_ref):
    for rows in _sub_blocks():
        h = _rms(x_ref[rows, :], g_ref[...], RMS_EPS).astype(BF16)
        a = jnp.dot(h, w_ref[:, :D_MODEL], preferred_element_type=F32) + b_ref[:, :D_MODEL]
        t = jnp.dot(h, w_ref[:, D_MODEL:], preferred_element_type=F32) + b_ref[:, D_MODEL:]
        u_ref[rows, :] = a * jax.nn.sigmoid(t)


def _glu(x, g, w_pw1, b_pw1, layer, ic):
    m = x.shape[0]
    row = pl.BlockSpec((FFN_ROW_TILE, D_MODEL), lambda i: (i, 0))
    return pl.pallas_call(
        _glu_kernel,
        grid=(m // FFN_ROW_TILE,),
        in_specs=[row, _const_spec((1, D_MODEL), (layer,)),
                  _const_spec((D_MODEL, 2 * D_MODEL), (ic,)),
                  _const_spec((1, 2 * D_MODEL), (ic,))],
        out_specs=row,
        out_shape=jax.ShapeDtypeStruct((m, D_MODEL), F32),
        compiler_params=_params(1),
        name="conv_glu",
    )(x, g, w_pw1, b_pw1)


def _conv_kernel(x_ref, u_ref, prev_ref, next_ref, wdw_ref, bdw_ref, lng_ref, lnb_ref,
                 w2_ref, b2_ref, gpost_ref, o_ref, ubuf_ref, sh_ref, c_ref, wrep_ref):
    i = pl.program_id(1)
    n = pl.num_programs(1)
    ubuf_ref[:HALO_ROWS, :] = jnp.where(i > 0, prev_ref[...], 0.0)
    ubuf_ref[HALO_ROWS:HALO_ROWS + ROW_TILE, :] = u_ref[...]
    ubuf_ref[HALO_ROWS + ROW_TILE:, :] = jnp.where(i < n - 1, next_ref[...], 0.0)

    @pl.when((pl.program_id(0) == 0) & (i == 0))
    def _replicate_taps():
        for k in range(CONV_WIDTH):
            wrep_ref[k] = jnp.broadcast_to(wdw_ref[k:k + 1, :], (SUBLANES, D_MODEL))

    groups = CONV_ROWS // SUBLANES
    for cb in range(D_MODEL // CONV_COLS):
        cols = slice(cb * CONV_COLS, (cb + 1) * CONV_COLS)
        for phase in range(1, SUBLANES):
            sh_ref[phase - 1] = ubuf_ref[phase:phase + SHIFT_ROWS, cols]

        def row_block(r, carry, cols=cols):
            base = pl.multiple_of(r * CONV_ROWS, CONV_ROWS)
            acc = [jnp.zeros((SUBLANES, CONV_COLS), F32) for _ in range(groups)]
            for k in range(CONV_WIDTH):
                step, phase = divmod(HALO_ROWS - CONV_PAD + k, SUBLANES)
                w = wrep_ref[k, :, cols]
                for g in range(groups):
                    rows = pl.ds(base + (step + g) * SUBLANES, SUBLANES)
                    win = ubuf_ref[rows, cols] if phase == 0 else sh_ref[phase - 1, rows, :]
                    acc[g] = acc[g] + win * w
            for g in range(groups):
                c_ref[pl.ds(base + g * SUBLANES, SUBLANES), cols] = acc[g]
            return carry

        lax.fori_loop(0, ROW_TILE // CONV_ROWS, row_block, 0)

    c = c_ref[...] + bdw_ref[...]
    mu = jnp.mean(c, axis=-1, keepdims=True)
    d = c - mu
    var = jnp.mean(d * d, axis=-1, keepdims=True)
    y = d * lax.rsqrt(var + LN_EPS) * lng_ref[...] + lnb_ref[...]
    y = (y * jax.nn.sigmoid(y)).astype(BF16)
    mix = jnp.dot(y, w2_ref[...], preferred_element_type=F32) + b2_ref[...]
    o_ref[...] = x_ref[...] + _rms(mix, gpost_ref[...], RMS_EPS)


def _conv(x, u, w_dw, b_dw, ln_g, ln_b, w_pw2, b_pw2, g_post, layer, ic):
    b, s, _ = x.shape
    tiles = s // ROW_TILE
    halo_per_tile = ROW_TILE // HALO_ROWS
    last_halo = s // HALO_ROWS - 1
    row = pl.BlockSpec((None, ROW_TILE, D_MODEL), lambda bi, i: (bi, i, 0))
    prev = pl.BlockSpec((None, HALO_ROWS, D_MODEL),
                        lambda bi, i: (bi, jnp.maximum(i * halo_per_tile - 1, 0), 0))
    nxt = pl.BlockSpec((None, HALO_ROWS, D_MODEL),
                       lambda bi, i: (bi, jnp.minimum((i + 1) * halo_per_tile, last_halo), 0))
    vec = _const_spec((1, D_MODEL), (ic,))
    return pl.pallas_call(
        _conv_kernel,
        grid=(b, tiles),
        in_specs=[row, row, prev, nxt, _const_spec((CONV_WIDTH, D_MODEL), (ic,)), vec, vec, vec,
                  _const_spec((D_MODEL, D_MODEL), (ic,)), vec,
                  _const_spec((1, D_MODEL), (layer,))],
        out_specs=row,
        out_shape=jax.ShapeDtypeStruct((b, s, D_MODEL), F32),
        scratch_shapes=[pltpu.VMEM((ROW_TILE + 2 * HALO_ROWS, D_MODEL), F32),
                        pltpu.VMEM((SUBLANES - 1, SHIFT_ROWS, CONV_COLS), F32),
                        pltpu.VMEM((ROW_TILE, D_MODEL), F32),
                        pltpu.VMEM((CONV_WIDTH, SUBLANES, D_MODEL), F32)],
        compiler_params=_params(2),
        name="conv_mix",
    )(x, u, u, u, w_dw, b_dw, ln_g, ln_b, w_pw2, b_pw2, g_post)


def _qkv_kernel(x_ref, g_ref, w_ref, o_ref):
    q_scale = HEAD_DIM ** -0.5 * LOG2_E
    for rows in _sub_blocks():
        h = _rms(x_ref[rows, :], g_ref[...], RMS_EPS).astype(BF16)
        for part in range(3):
            cols = slice(part * QK_WIDTH, (part + 1) * QK_WIDTH)
            y = jnp.dot(h, w_ref[:, cols], preferred_element_type=F32)
            if part == 0:
                y = y * q_scale
            o_ref[rows, cols] = y.astype(BF16)


def _qkv(x, g, w_qkv, layer, ia):
    m = x.shape[0]
    return pl.pallas_call(
        _qkv_kernel,
        grid=(m // FFN_ROW_TILE,),
        in_specs=[pl.BlockSpec((FFN_ROW_TILE, D_MODEL), lambda i: (i, 0)),
                  _const_spec((1, D_MODEL), (layer,)),
                  _const_spec((D_MODEL, 3 * QK_WIDTH), (ia,))],
        out_specs=pl.BlockSpec((FFN_ROW_TILE, 3 * QK_WIDTH), lambda i: (i, 0)),
        out_shape=jax.ShapeDtypeStruct((m, 3 * QK_WIDTH), BF16),
        compiler_params=_params(1),
        name="attn_qkv",
    )(x, g, w_qkv)


def _bias_lanes(lane_rel, pos, coef):
    piece = lane_rel % BIAS_SPLIT
    coef_lane = jnp.where(piece == 0, coef[0], jnp.where(piece == 1, coef[1], coef[2]))
    pos_lo = (pos % POS_RADIX).astype(F32)
    pos_hi = (pos - pos % POS_RADIX).astype(F32)
    pos_lane = jnp.where(lane_rel % (2 * BIAS_SPLIT) < BIAS_SPLIT, pos_lo, pos_hi)
    return pos_lane, coef_lane


def _attn_kernel(slope_ref, lam_ref, subg_ref, q_ref, k_ref, v_ref, o_ref,
                 vt_ref, ka_ref, diag_ref, acc_ref, *, lambda_init, seq):
    step = pl.program_id(2)
    n_chunks = seq // K_TILE
    coef = slope_ref[:, :1] * LOG2_E
    pieces = []
    rest = coef
    for _ in range(BIAS_SPLIT):
        piece = rest.astype(BF16).astype(F32)
        pieces.append(piece)
        rest = rest - piece

    def side_lane_index(shape, mp):
        lane = lax.broadcasted_iota(jnp.int32, shape, 1)
        lane_rel = lane - (HEAD_DIM if mp == 0 else 0)
        return lane, lane_rel, (lane_rel >= 0) & (lane_rel < 4 * BIAS_SPLIT)

    @pl.when(step == 0)
    def _per_head_setup():
        ones_row = lax.broadcasted_iota(jnp.int32, (BF16_ROWS, K_TILE), 0) == 0
        for c in range(n_chunks):
            vt_ref[c, :HEAD_WIDTH] = v_ref[c * K_TILE:(c + 1) * K_TILE, :].astype(F32).T.astype(BF16)
            vt_ref[c, HEAD_WIDTH:] = ones_row.astype(BF16)
        jpos = lax.broadcasted_iota(jnp.int32, (K_TILE, HEAD_WIDTH), 0)
        for mp in range(2):
            lane, lane_rel, is_side = side_lane_index((K_TILE, HEAD_WIDTH), mp)
            pos_lane, coef_lane = _bias_lanes(lane_rel, jpos, pieces)
            side = jnp.where(lane_rel < 2 * BIAS_SPLIT, coef_lane, pos_lane)
            side = jnp.where(is_side, side, 0.0).astype(BF16)
            own = (lane < HEAD_DIM) if mp == 0 else (lane >= HEAD_DIM)
            for c in range(n_chunks):
                k = k_ref[c * K_TILE:(c + 1) * K_TILE, :]
                ka_ref[mp, c] = jnp.where(own, k, side)
        rel = (lax.broadcasted_iota(jnp.int32, (K_TILE, Q_TILE), 0)
               - lax.broadcasted_iota(jnp.int32, (K_TILE, Q_TILE), 1))
        diag_ref[...] = jnp.abs(rel).astype(F32) * coef

    def query_operands(q):
        ipos = lax.broadcasted_iota(jnp.int32, q.shape, 0)
        q_own, q_after, q_before = [], [], []
        for mp in range(2):
            lane, lane_rel, is_side = side_lane_index(q.shape, mp)
            pos_lane, coef_lane = _bias_lanes(lane_rel, ipos, pieces)
            side = jnp.where(lane_rel < 2 * BIAS_SPLIT, pos_lane, -coef_lane)
            side = jnp.where(is_side, side, 0.0)
            own = (lane < HEAD_DIM) if mp == 0 else (lane >= HEAD_DIM)
            q_own.append(jnp.where(own, q, jnp.zeros_like(q)))
            q_after.append(jnp.where(own, q, side.astype(BF16)))
            q_before.append(jnp.where(own, q, (-side).astype(BF16)))
        return q_own, q_after, q_before

    def fast_tile(tile, q):
        q_own, q_after, q_before = query_operands(q)

        def scores(t):
            chunk = tile + t
            wrapped = chunk >= n_chunks
            chunk = jnp.where(wrapped, chunk - n_chunks, chunk)
            dist = jnp.where(wrapped, n_chunks - t, t)
            const = coef * (dist * K_TILE).astype(F32)
            out = []
            for mp in range(2):
                if t == 0:
                    qm = q_own[mp]
                else:
                    qm = jnp.where(wrapped, q_before[mp], q_after[mp])
                s = lax.dot_general(ka_ref[mp, chunk], qm, (((1,), (1,)), ((), ())),
                                    preferred_element_type=F32)
                if t == 0:
                    s = s - diag_ref[...]
                out.append(s)
            return out, const, chunk

        ahead = scores(0)
        ref_max = [jnp.max(s, axis=0, keepdims=True) for s in ahead[0]]
        acc = [None, None]
        for t in range(n_chunks):
            s_cur, const, chunk = ahead
            if t + 1 < n_chunks:
                ahead = scores(t + 1)
            vt = vt_ref[chunk]
            for mp in range(2):
                p = jnp.exp2(s_cur[mp] - (ref_max[mp] + const)).astype(BF16)
                pv = jnp.dot(vt, p, preferred_element_type=F32)
                acc[mp] = pv if t == 0 else acc[mp] + pv
        return acc

    def exact_tile(tile, q):
        q_own = query_operands(q)[0]
        kpos = lax.broadcasted_iota(jnp.int32, (K_TILE, Q_TILE), 0)
        qpos = lax.broadcasted_iota(jnp.int32, (K_TILE, Q_TILE), 1) + tile * Q_TILE

        def chunk_scores(c):
            start = pl.multiple_of(c * K_TILE, K_TILE)
            k = k_ref[pl.ds(start, K_TILE), :]
            bias = jnp.abs(kpos + start - qpos).astype(F32) * coef
            return [lax.dot_general(k, qm, (((1,), (1,)), ((), ())),
                                    preferred_element_type=F32) - bias for qm in q_own]

        def max_pass(c, m):
            return tuple(jnp.maximum(mm, jnp.max(s, axis=0, keepdims=True))
                         for mm, s in zip(m, chunk_scores(c)))

        neg_inf = jnp.full((1, Q_TILE), -jnp.inf, F32)
        true_max = lax.fori_loop(0, n_chunks, max_pass, (neg_inf, neg_inf))
        acc_ref[...] = jnp.zeros(acc_ref.shape, F32)

        def exp_pass(c, carry):
            for mp, s in enumerate(chunk_scores(c)):
                p = jnp.exp2(s - true_max[mp]).astype(BF16)
                acc_ref[mp] += jnp.dot(vt_ref[c], p, preferred_element_type=F32)
            return carry

        lax.fori_loop(0, n_chunks, exp_pass, 0)
        return [acc_ref[0], acc_ref[1]]

    def finish(acc, rows):
        lq = lam_ref[...]
        lam = (jnp.exp(jnp.sum(lq[0:1] * lq[1:2], axis=-1, keepdims=True))
               - jnp.exp(jnp.sum(lq[2:3] * lq[3:4], axis=-1, keepdims=True)) + lambda_init)
        ot = [a[:HEAD_WIDTH] / a[HEAD_WIDTH:HEAD_WIDTH + 1] for a in acc]
        o = (ot[0] - lam * ot[1]).T
        o = _rms(o, subg_ref[...], SUBLN_EPS) * (1.0 - lambda_init)
        o_ref[rows, :] = o.astype(BF16)

    overflow = []
    for j in range(TILES_PER_STEP):
        rows = slice(j * Q_TILE, (j + 1) * Q_TILE)
        acc = fast_tile(step * TILES_PER_STEP + j, q_ref[rows, :])
        finish(acc, rows)
        bad = jnp.logical_not((jnp.abs(acc[0]) < F32_HUGE) & (jnp.abs(acc[1]) < F32_HUGE))
        overflow.append(jnp.max(bad.astype(F32)) > 0.0)

    for j in range(TILES_PER_STEP):
        rows = slice(j * Q_TILE, (j + 1) * Q_TILE)

        @pl.when(overflow[j])
        def _redo_exactly(j=j, rows=rows):
            finish(exact_tile(step * TILES_PER_STEP + j, q_ref[rows, :]), rows)


def _attn(qkv, slopes, lam_vecs, subln_g, lambda_init, ia):
    b, s, _ = qkv.shape
    kernel = functools.partial(_attn_kernel, lambda_init=lambda_init, seq=s)
    step_rows = TILES_PER_STEP * Q_TILE
    return pl.pallas_call(
        kernel,
        grid=(b, N_HEADS, s // step_rows),
        in_specs=[pl.BlockSpec((None, 1, HEAD_WIDTH), lambda bi, h, i: (h, 0, 0)),
                  _const_spec((4, HEAD_DIM), (ia,)), _const_spec((1, HEAD_WIDTH), (ia,)),
                  pl.BlockSpec((None, step_rows, HEAD_WIDTH), lambda bi, h, i: (bi, i, h)),
                  pl.BlockSpec((None, s, HEAD_WIDTH), lambda bi, h, i: (bi, 0, N_HEADS + h)),
                  pl.BlockSpec((None, s, HEAD_WIDTH), lambda bi, h, i: (bi, 0, 2 * N_HEADS + h))],
        out_specs=pl.BlockSpec((None, step_rows, HEAD_WIDTH), lambda bi, h, i: (bi, i, h)),
        out_shape=jax.ShapeDtypeStruct((b, s, QK_WIDTH), BF16),
        scratch_shapes=[pltpu.VMEM((s // K_TILE, HEAD_WIDTH + BF16_ROWS, K_TILE), BF16),
                        pltpu.VMEM((2, s // K_TILE, K_TILE, HEAD_WIDTH), BF16),
                        pltpu.VMEM((K_TILE, Q_TILE), F32),
                        pltpu.VMEM((2, HEAD_WIDTH + BF16_ROWS, Q_TILE), F32)],
        compiler_params=_params(3),
        name="attn_core",
    )(slopes, lam_vecs, subln_g, qkv, qkv, qkv)


def _oproj_kernel(x_ref, o_ref, w_ref, g_ref, y_ref):
    for rows in _sub_blocks():
        mix = jnp.dot(o_ref[rows, :], w_ref[...], preferred_element_type=F32)
        y_ref[rows, :] = x_ref[rows, :] + _rms(mix, g_ref[...], RMS_EPS)


def _oproj(x, o, w_o, g_post, layer, ia):
    m = x.shape[0]
    row = pl.BlockSpec((FFN_ROW_TILE, D_MODEL), lambda i: (i, 0))
    return pl.pallas_call(
        _oproj_kernel,
        grid=(m // FFN_ROW_TILE,),
        in_specs=[row, row, _const_spec((QK_WIDTH, D_MODEL), (ia,)),
                  _const_spec((1, D_MODEL), (layer,))],
        out_specs=row,
        out_shape=jax.ShapeDtypeStruct((m, D_MODEL), F32),
        compiler_params=_params(1),
        name="attn_oproj",
    )(x, o, w_o, g_post)


def _lambda_init(layer_idx):
    return 0.8 - 0.6 * math.exp(-0.3 * layer_idx)


def kernel(x, ffn_norm_pre, ffn_norm_post, ffn_w_gate, ffn_w_up, ffn_w_down, mix_norm_pre, mix_norm_post, conv_w_pw1, conv_b_pw1, conv_w_dw, conv_b_dw, conv_ln_g, conv_ln_b, conv_w_pw2, conv_b_pw2, attn_w_qkv, attn_w_o, attn_lambda_q1, attn_lambda_k1, attn_lambda_q2, attn_lambda_k2, attn_subln_g):
    b, s, d = x.shape
    depth = ffn_norm_pre.shape[0]
    m = b * s
    assert d == D_MODEL and m % ROW_TILE == 0 and s % ROW_TILE == 0 and m % FFN_ROW_TILE == 0
    assert s % (TILES_PER_STEP * Q_TILE) == 0 and s % K_TILE == 0

    w_gate, w_up, w_down = (w.astype(BF16) for w in (ffn_w_gate, ffn_w_up, ffn_w_down))
    w_pw1, w_pw2 = conv_w_pw1.astype(BF16), conv_w_pw2.astype(BF16)
    w_qkv, w_o = attn_w_qkv.astype(BF16), attn_w_o.astype(BF16)
    g_ffn_pre, g_ffn_post = ffn_norm_pre[:, :, None, :], ffn_norm_post[:, :, None, :]
    g_mix_pre, g_mix_post = mix_norm_pre[:, None, :], mix_norm_post[:, None, :]
    b_pw1, b_dw, b_pw2 = conv_b_pw1[:, None, :], conv_b_dw[:, None, :], conv_b_pw2[:, None, :]
    ln_g, ln_b = conv_ln_g[:, None, :], conv_ln_b[:, None, :]
    subln_g = attn_subln_g[:, None, :]
    lam_vecs = jnp.stack([attn_lambda_q1, attn_lambda_k1, attn_lambda_q2, attn_lambda_k2], axis=1)
    slopes = 2.0 ** (-8.0 * (jnp.arange(N_HEADS, dtype=F32) + 1.0) / N_HEADS)
    slopes = jnp.broadcast_to(slopes[:, None, None], (N_HEADS, 1, HEAD_WIDTH))

    x = x.reshape(m, d)
    ic = ia = 0
    for i in range(depth):
        x = _ffn(x, g_ffn_pre, g_ffn_post, w_gate, w_up, w_down, (i, 0))
        if i % 2 == 0:
            u = _glu(x, g_mix_pre, w_pw1, b_pw1, i, ic)
            x = _conv(x.reshape(b, s, d), u.reshape(b, s, d), conv_w_dw, b_dw, ln_g, ln_b,
                      w_pw2, b_pw2, g_mix_post, i, ic).reshape(m, d)
            ic += 1
        else:
            qkv = _qkv(x, g_mix_pre, w_qkv, i, ia)
            o = _attn(qkv.reshape(b, s, 3 * QK_WIDTH), slopes, lam_vecs, subln_g,
                      _lambda_init(i), ia)
            x = _oproj(x, o.reshape(m, QK_WIDTH), w_o, g_mix_post, i, ia)
            ia += 1
        x = _ffn(x, g_ffn_pre, g_ffn_post, w_gate, w_up, w_down, (i, 1))
    return x.reshape(b, s, d)
```
